```python
import jax, jax.numpy as jnp
from jax import lax
import numpy as np

D_MODEL = 1024
BATCH = 4
SEQ = 8192
DEPTH = 1

ATTN_HEADS = 8
HEAD_DIM = 64
ATTN_WIDTH = ATTN_HEADS * HEAD_DIM
MOBA_BLOCK = 256
MOBA_TOPK = 3
Q_CHUNK = 32
LRU_WIDTH = D_MODEL - ATTN_WIDTH
LRU_HEADS = 8
LRU_HEAD_DIM = LRU_WIDTH // LRU_HEADS
LRU_CONV = 4
LRU_C = 8.0
N_IN = 3 * ATTN_WIDTH + 2 * LRU_WIDTH
D_FF = 2816
FFN_CONV = 3
N_MOD = 6
EPS = 1e-6

kernel_name = "hymba_rglru_moba_convffn_adaln"


def rmsnorm(x, g):
    xf = x.astype(jnp.float32)
    y = xf * lax.rsqrt(jnp.mean(xf * xf, axis=-1, keepdims=True) + EPS)
    return (y * g.astype(jnp.float32)).astype(x.dtype)


def causal_dwconv(x, w, b):
    K = w.shape[0]
    S = x.shape[1]
    xp = jnp.pad(x, ((0, 0), (K - 1, 0), (0, 0)))
    y = xp[:, 0:S] * w[0]
    for j in range(1, K):
        y = y + xp[:, j:j + S] * w[j]
    return y + b


def rg_lru(xb, w_a, b_a, w_x, b_x, lam):
    B, S, _ = xb.shape
    xh = xb.reshape(B, S, LRU_HEADS, LRU_HEAD_DIM)
    r = jax.nn.sigmoid(jnp.einsum('bshi,hij->bshj', xh, w_a).reshape(B, S, LRU_WIDTH) + b_a)
    i = jax.nn.sigmoid(jnp.einsum('bshi,hij->bshj', xh, w_x).reshape(B, S, LRU_WIDTH) + b_x)
    log_a = -LRU_C * r.astype(jnp.float32) * jax.nn.softplus(-lam.astype(jnp.float32))
    a = jnp.exp(log_a)
    u = jnp.sqrt(-jnp.expm1(2.0 * log_a)) * (i * xb).astype(jnp.float32)

    def combine(left, right):
        a1, b1 = left
        a2, b2 = right
        return a1 * a2, a2 * b1 + b2

    _, h = lax.associative_scan(combine, (a, u), axis=1)
    return h.astype(xb.dtype)


def moba_attention(q, k, v):
    B, S, H, Dh = q.shape
    nb = -(-S // MOBA_BLOCK)
    s_pad = nb * MOBA_BLOCK
    pad = ((0, 0), (0, s_pad - S), (0, 0), (0, 0))
    q, k, v = [jnp.pad(t, pad).transpose(0, 2, 1, 3) for t in (q, k, v)]
    k_blk = k.reshape(B, H, nb, MOBA_BLOCK, Dh)
    v_blk = v.reshape(B, H, nb, MOBA_BLOCK, Dh)
    k_mean = jnp.mean(k_blk.astype(jnp.float32), axis=3)
    gate = jnp.einsum('bhsd,bhnd->bhsn', q.astype(jnp.float32), k_mean)
    q_blk_id = jnp.arange(s_pad) // MOBA_BLOCK
    past = jnp.arange(nb)[None, :] < q_blk_id[:, None]
    gate = jnp.where(past, gate, -jnp.inf)
    n_sel = min(MOBA_TOPK, nb)
    top_v, top_i = lax.top_k(gate, n_sel)
    sel_valid = jnp.isfinite(top_v)
    own = jnp.broadcast_to(q_blk_id[None, None, :, None], (B, H, s_pad, 1)).astype(top_i.dtype)
    blk_idx = jnp.concatenate([top_i, own], axis=-1)
    blk_valid = jnp.concatenate([sel_valid, jnp.ones(own.shape, dtype=bool)], axis=-1)

    nc = s_pad // Q_CHUNK

    def to_chunks(t):
        return jnp.moveaxis(t.reshape(B, H, nc, Q_CHUNK, *t.shape[3:]), 2, 0)

    bi = jnp.arange(B)[:, None, None, None]
    hi = jnp.arange(H)[None, :, None, None]
    offs = jnp.arange(MOBA_BLOCK)
    scale = Dh ** -0.5

    def chunk_attn(args):
        c_idx, qc, idx, valid = args
        kg = k_blk[bi, hi, idx]
        vg = v_blk[bi, hi, idx]
        s = jnp.einsum('bhqd,bhqnkd->bhqnk', qc, kg).astype(jnp.float32) * scale
        q_pos = c_idx * Q_CHUNK + jnp.arange(Q_CHUNK)
        k_pos = idx[..., None] * MOBA_BLOCK + offs
        mask = valid[..., None] & (k_pos <= q_pos[:, None, None])
        s = jnp.where(mask, s, -jnp.inf)
        p = jax.nn.softmax(s.reshape(B, H, Q_CHUNK, -1), axis=-1).reshape(s.shape)
        return jnp.einsum('bhqnk,bhqnkd->bhqd', p.astype(vg.dtype), vg)

    out = lax.map(chunk_attn, (jnp.arange(nc), to_chunks(q), to_chunks(blk_idx), to_chunks(blk_valid)))
    out = jnp.moveaxis(out, 0, 2).reshape(B, H, s_pad, Dh)[:, :, :S]
    return out.transpose(0, 2, 1, 3).reshape(B, S, H * Dh)


def setup_inputs(seed: int = 0) -> dict:
    key = jax.random.key(seed)
    ks = jax.random.split(key, 24)
    f32 = jnp.float32
    nrm = lambda k, shape, s: jax.random.normal(k, shape, f32) * s
    gain = lambda k, shape: 1.0 + 0.02 * jax.random.normal(k, shape, f32)
    u = jax.random.uniform(ks[14], (DEPTH, LRU_WIDTH), f32, 0.9, 0.999)
    a_base = u ** (1.0 / LRU_C)
    lru_lambda = jnp.log(a_base) - jnp.log1p(-a_base)
    return {
        "x": jax.random.normal(ks[0], (BATCH, SEQ, D_MODEL), f32),
        "c": jax.random.normal(ks[1], (BATCH, D_MODEL), f32),
        "w_ada": nrm(ks[2], (DEPTH, D_MODEL, N_MOD * D_MODEL), 0.5 * D_MODEL ** -0.5),
        "b_ada": nrm(ks[3], (DEPTH, N_MOD * D_MODEL), 0.02),
        "norm1_g": gain(ks[4], (DEPTH, D_MODEL)),
        "w_in": nrm(ks[5], (DEPTH, D_MODEL, N_IN), D_MODEL ** -0.5),
        "q_norm_g": gain(ks[6], (DEPTH, HEAD_DIM)),
        "k_norm_g": gain(ks[7], (DEPTH, HEAD_DIM)),
        "lru_conv_w": nrm(ks[8], (DEPTH, LRU_CONV, LRU_WIDTH), LRU_CONV ** -0.5),
        "lru_conv_b": nrm(ks[9], (DEPTH, LRU_WIDTH), 0.02),
        "lru_wa": nrm(ks[10], (DEPTH, LRU_HEADS, LRU_HEAD_DIM, LRU_HEAD_DIM), LRU_HEAD_DIM ** -0.5),
        "lru_ba": nrm(ks[11], (DEPTH, LRU_WIDTH), 0.02),
        "lru_wx": nrm(ks[12], (DEPTH, LRU_HEADS, LRU_HEAD_DIM, LRU_HEAD_DIM), LRU_HEAD_DIM ** -0.5),
        "lru_bx": nrm(ks[13], (DEPTH, LRU_WIDTH), 0.02),
        "lru_lambda": lru_lambda,
        "lru_out_g": gain(ks[15], (DEPTH, LRU_WIDTH)),
        "attn_out_g": gain(ks[16], (DEPTH, ATTN_WIDTH)),
        "w_out": nrm(ks[17], (DEPTH, D_MODEL, D_MODEL), D_MODEL ** -0.5),
        "norm2_g": gain(ks[18], (DEPTH, D_MODEL)),
        "w_up": nrm(ks[19], (DEPTH, D_MODEL, 2 * D_FF), D_MODEL ** -0.5),
        "ffn_conv_w": nrm(ks[20], (DEPTH, FFN_CONV, 2 * D_FF), FFN_CONV ** -0.5),
        "ffn_conv_b": nrm(ks[21], (DEPTH, 2 * D_FF), 0.02),
        "w_down": nrm(ks[22], (DEPTH, D_FF, D_MODEL), D_FF ** -0.5),
    }


def reference(x, c, w_ada, b_ada, norm1_g, w_in, q_norm_g, k_norm_g, lru_conv_w, lru_conv_b,
              lru_wa, lru_ba, lru_wx, lru_bx, lru_lambda, lru_out_g, attn_out_g, w_out,
              norm2_g, w_up, ffn_conv_w, ffn_conv_b, w_down):
    B, S, D = x.shape
    for l in range(DEPTH):
        mod = (c @ w_ada[l] + b_ada[l])[:, None, :]
        sh1, sc1, g1, sh2, sc2, g2 = jnp.split(mod, N_MOD, axis=-1)

        h = rmsnorm(x, norm1_g[l]) * (1.0 + sc1) + sh1
        z = h @ w_in[l]
        q, k, v, xr, gr = jnp.split(
            z, [ATTN_WIDTH, 2 * ATTN_WIDTH, 3 * ATTN_WIDTH, 3 * ATTN_WIDTH + LRU_WIDTH], axis=-1)
        q = rmsnorm(q.reshape(B, S, ATTN_HEADS, HEAD_DIM), q_norm_g[l])
        k = rmsnorm(k.reshape(B, S, ATTN_HEADS, HEAD_DIM), k_norm_g[l])
        v = v.reshape(B, S, ATTN_HEADS, HEAD_DIM)
        attn = moba_attention(q, k, v)
        xr = causal_dwconv(xr, lru_conv_w[l], lru_conv_b[l])
        lru = rg_lru(xr, lru_wa[l], lru_ba[l], lru_wx[l], lru_bx[l], lru_lambda[l]) * jax.nn.gelu(gr)
        mix = jnp.concatenate([rmsnorm(lru, lru_out_g[l]), rmsnorm(attn, attn_out_g[l])], axis=-1)
        x = x + g1 * (mix @ w_out[l])

        h2 = rmsnorm(x, norm2_g[l]) * (1.0 + sc2) + sh2
        up = causal_dwconv(h2 @ w_up[l], ffn_conv_w[l], ffn_conv_b[l])
        gate, val = jnp.split(up, 2, axis=-1)
        x = x + g2 * ((jax.nn.silu(gate) * val) @ w_down[l])
    return x
```

```python
import functools
import math

import jax
import jax.numpy as jnp
from jax import lax
from jax.experimental import pallas as pl
from jax.experimental.pallas import tpu as pltpu

D_MODEL = 1024
ATTN_HEADS = 8
HEAD_DIM = 64
ATTN_WIDTH = ATTN_HEADS * HEAD_DIM
MOBA_BLOCK = 256
MOBA_TOPK = 3
LRU_WIDTH = D_MODEL - ATTN_WIDTH
LRU_HEADS = 8
LRU_CONV = 4
LRU_C = 8.0
N_IN = 3 * ATTN_WIDTH + 2 * LRU_WIDTH
D_FF = 2816
FFN_CONV = 3
N_MOD = 6
EPS = 1e-6

V7X_LANES = 128
V7X_SUBLANES = 8
V7X_VMEM_LIMIT_BYTES = 56 * 1024 * 1024

HEADS_PER_PAIR = V7X_LANES // HEAD_DIM
N_PAIRS = ATTN_HEADS // HEADS_PER_PAIR
MASK_NEG = -1e30

TM_IN = 256
TM_OUT = 256
KV_CHUNK_BLOCKS = 2
FF_CHUNK = 1408
MOD_CHUNK = 1536

BF16 = jnp.bfloat16
F32 = jnp.float32


def _dot(a, b):
    return jnp.dot(a, b, preferred_element_type=F32)


def _dot_nt(a, b):
    return lax.dot_general(a, b, (((1,), (1,)), ((), ())), preferred_element_type=F32)


def _rms_scale(y):
    return lax.rsqrt(jnp.mean(y * y, axis=-1, keepdims=True) + EPS)


def _mod_kernel(c_ref, w_ref, b_ref, o_ref):
    o_ref[...] = _dot(c_ref[...], w_ref[...]) + b_ref[...]


def _modulation(c, w_ada, b_ada):
    B = c.shape[0]
    n = w_ada.shape[1]
    return pl.pallas_call(
        _mod_kernel,
        grid=(n // MOD_CHUNK,),
        in_specs=[
            pl.BlockSpec((B, D_MODEL), lambda j: (0, 0)),
            pl.BlockSpec((D_MODEL, MOD_CHUNK), lambda j: (0, j)),
            pl.BlockSpec((1, MOD_CHUNK), lambda j: (0, j)),
        ],
        out_specs=pl.BlockSpec((B, MOD_CHUNK), lambda j: (0, j)),
        out_shape=jax.ShapeDtypeStruct((B, n), F32),
        compiler_params=pltpu.CompilerParams(
            dimension_semantics=("arbitrary",), vmem_limit_bytes=V7X_VMEM_LIMIT_BYTES),
        name="modulation",
    )(c, w_ada, b_ada)


def _head_rmsnorm(t, gmat, gain):
    sq = t * t
    hi = sq.astype(BF16)
    lo = (sq - hi.astype(F32)).astype(BF16)
    ms = _dot(hi, gmat) + _dot(lo, gmat)
    return t * lax.rsqrt(ms + EPS) * gain


def _shift_rows(t, d, fill):
    tm = t.shape[0]
    if d % V7X_SUBLANES == 0:
        pad = jnp.full((d, t.shape[1]), fill, t.dtype)
        return jnp.concatenate([pad, t[: tm - d]], axis=0)
    row = lax.broadcasted_iota(jnp.int32, t.shape, 0)
    return jnp.where(row >= d, pltpu.roll(t, d, 0), fill)


def _linear_scan(a, u):
    tm = a.shape[0]
    d = 1
    while d < tm:
        u = a * _shift_rows(u, d, 0.0) + u
        a = a * _shift_rows(a, d, 1.0)
        d *= 2
    return a, u


def _gelu_tanh(t):
    return 0.5 * t * (1.0 + jnp.tanh(math.sqrt(2.0 / math.pi) * (t + 0.044715 * (t * t * t))))


def _mixer_in_kernel(x_ref, mod_ref, g1_ref, w_in_ref, gq_ref, gk_ref, gmat_ref, cw_ref, cb_ref,
                     wg_ref, bg_ref, lam_ref, go_ref,
                     q_ref, k_ref, v_ref, km_ref, lru_ref,
                     xr_ext, h_carry):
    tm = x_ref.shape[1]
    first = pl.program_id(1) == 0

    @pl.when(first)
    def _():
        xr_ext[0:V7X_SUBLANES, :] = jnp.zeros((V7X_SUBLANES, LRU_WIDTH), F32)
        h_carry[...] = jnp.zeros_like(h_carry)

    x = x_ref[0]
    mod = mod_ref[0]
    sh1 = mod[0:1, :]
    sc1 = mod[1:2, :]
    h = (x * _rms_scale(x) * g1_ref[...]) * (1.0 + sc1) + sh1
    hb = h.astype(BF16)

    def proj(j):
        return _dot(hb, w_in_ref[:, j * ATTN_WIDTH:(j + 1) * ATTN_WIDTH])

    gmat = gmat_ref[...]
    qn = _head_rmsnorm(proj(0), gmat, gq_ref[...])
    q_ref[0] = (qn * (HEAD_DIM ** -0.5)).astype(BF16)
    kn = _head_rmsnorm(proj(1), gmat, gk_ref[...])
    k_ref[0] = kn.astype(BF16)
    for blk in range(tm // MOBA_BLOCK):
        km_ref[0, blk] = jnp.mean(kn[blk * MOBA_BLOCK:(blk + 1) * MOBA_BLOCK], axis=0, keepdims=True)
    v_ref[0] = proj(2).astype(BF16)

    xr_ext[V7X_SUBLANES:, :] = proj(3)
    xc = cb_ref[...] + cw_ref[LRU_CONV - 1:LRU_CONV, :] * xr_ext[V7X_SUBLANES:, :]
    for j in range(LRU_CONV - 1):
        back = LRU_CONV - 1 - j
        xc = xc + cw_ref[j:j + 1, :] * xr_ext[pl.ds(V7X_SUBLANES - back, tm), :]
    xr_ext[0:V7X_SUBLANES, :] = xr_ext[tm:tm + V7X_SUBLANES, :]

    gates = _dot(xc.astype(BF16), wg_ref[...]) + bg_ref[...]
    r = jax.nn.sigmoid(gates[:, :LRU_WIDTH])
    i = jax.nn.sigmoid(gates[:, LRU_WIDTH:])
    neg_lam = -lam_ref[...]
    softplus = jnp.maximum(neg_lam, 0.0) + jnp.log1p(jnp.exp(-jnp.abs(neg_lam)))
    log_a = (-LRU_C) * r * softplus
    a = jnp.exp(log_a)
    th = jnp.tanh(log_a)
    u = jnp.sqrt(-2.0 * th / (1.0 - th)) * (i * xc)
    a_run, h_loc = _linear_scan(a, u)
    hs = h_loc + a_run * h_carry[0:1, :]
    h_carry[0:1, :] = hs[tm - 1:tm, :]

    y = hs * _gelu_tanh(proj(4))
    lru_ref[0] = (y * _rms_scale(y) * go_ref[...]).astype(BF16)


def _mixer_in(x, mod, g1, w_in, gq, gk, gmat, cw, cb, wg, bg, lam, go):
    B, S, _ = x.shape
    tm = TM_IN
    nb = S // MOBA_BLOCK
    const = lambda shape: pl.BlockSpec(shape, lambda b, s: (0,) * len(shape))
    tok = lambda w: pl.BlockSpec((1, tm, w), lambda b, s: (b, s, 0))
    return pl.pallas_call(
        _mixer_in_kernel,
        grid=(B, S // tm),
        in_specs=[
            tok(D_MODEL),
            pl.BlockSpec((1, N_MOD, D_MODEL), lambda b, s: (b, 0, 0)),
            const((1, D_MODEL)),
            const((D_MODEL, N_IN)),
            const((1, ATTN_WIDTH)),
            const((1, ATTN_WIDTH)),
            const((ATTN_WIDTH, ATTN_WIDTH)),
            const((LRU_CONV, LRU_WIDTH)),
            const((1, LRU_WIDTH)),
            const((LRU_WIDTH, 2 * LRU_WIDTH)),
            const((1, 2 * LRU_WIDTH)),
            const((1, LRU_WIDTH)),
            const((1, LRU_WIDTH)),
        ],
        out_specs=[
            tok(ATTN_WIDTH),
            tok(ATTN_WIDTH),
            tok(ATTN_WIDTH),
            pl.BlockSpec((1, tm // MOBA_BLOCK, 1, ATTN_WIDTH), lambda b, s: (b, s, 0, 0)),
            tok(LRU_WIDTH),
        ],
        out_shape=[
            jax.ShapeDtypeStruct((B, S, ATTN_WIDTH), BF16),
            jax.ShapeDtypeStruct((B, S, ATTN_WIDTH), BF16),
            jax.ShapeDtypeStruct((B, S, ATTN_WIDTH), BF16),
            jax.ShapeDtypeStruct((B, nb, 1, ATTN_WIDTH), F32),
            jax.ShapeDtypeStruct((B, S, LRU_WIDTH), BF16),
        ],
        scratch_shapes=[
            pltpu.VMEM((tm + V7X_SUBLANES, LRU_WIDTH), F32),
            pltpu.VMEM((V7X_SUBLANES, LRU_WIDTH), F32),
        ],
        compiler_params=pltpu.CompilerParams(
            dimension_semantics=("arbitrary", "arbitrary"), vmem_limit_bytes=V7X_VMEM_LIMIT_BYTES),
        name="mixer_in",
    )(x, mod, g1, w_in, gq, gk, gmat, cw, cb, wg, bg, lam, go)


def _select_bias(gate, n_past):
    blk = lax.broadcasted_iota(jnp.int32, gate.shape, 1)
    neg_inf = jnp.float32(-jnp.inf)
    g = jnp.where(blk < n_past, gate, neg_inf)
    sel = jnp.zeros(gate.shape, jnp.bool_)
    for _ in range(MOBA_TOPK):
        mx = jnp.max(g, axis=1, keepdims=True)
        cand = jnp.where((g == mx) & (mx > neg_inf), blk, V7X_LANES)
        pick = blk == jnp.min(cand, axis=1, keepdims=True)
        sel = sel | pick
        g = jnp.where(pick, neg_inf, g)
    return jnp.where(sel, 0.0, MASK_NEG)


def _moba_kernel(q_ref, k_ref, v_ref, km_ref, o_ref, m_scr, acc_scr):
    tq = q_ref.shape[1]
    i = pl.program_id(2)
    lane = lax.broadcasted_iota(jnp.int32, (1, V7X_LANES), 1)
    head_lanes = [(lane >= h * HEAD_DIM) & (lane < (h + 1) * HEAD_DIM) for h in range(HEADS_PER_PAIR)]
    den_lane = [HEAD_DIM, 0]

    q = q_ref[0]
    km = km_ref[0].astype(BF16)
    km = jnp.concatenate([km, jnp.zeros((V7X_LANES - km.shape[0], V7X_LANES), BF16)], axis=0)
    zero = jnp.zeros_like(q)

    def v_ext(vc, h):
        ones = jnp.where(lane == den_lane[h], 1.0, 0.0).astype(BF16)
        return jnp.where(head_lanes[h], vc, ones)

    row = lax.broadcasted_iota(jnp.int32, (tq, MOBA_BLOCK), 0)
    col = lax.broadcasted_iota(jnp.int32, (tq, MOBA_BLOCK), 1)
    start = pl.multiple_of(i * MOBA_BLOCK, MOBA_BLOCK)
    kd = k_ref[0, pl.ds(start, MOBA_BLOCK), :]
    vd = v_ref[0, pl.ds(start, MOBA_BLOCK), :]

    qx = []
    for h in range(HEADS_PER_PAIR):
        qh = jnp.where(head_lanes[h], q, zero)
        bias = _select_bias(_dot_nt(qh, km), i).astype(BF16)
        qx.append(jnp.concatenate([qh, bias], axis=1))
        s = jnp.where(col <= row, _dot_nt(qh, kd), MASK_NEG)
        m = jnp.max(s, axis=1, keepdims=True)
        p = jnp.exp(s - m)
        m_scr[h] = jnp.broadcast_to(m, (tq, V7X_LANES))
        acc_scr[h] = _dot(p.astype(BF16), v_ext(vd, h))

    ck = KV_CHUNK_BLOCKS * MOBA_BLOCK
    blk_of_row = lax.broadcasted_iota(jnp.int32, (ck, V7X_LANES), 0) // MOBA_BLOCK
    lane_ck = lax.broadcasted_iota(jnp.int32, (ck, V7X_LANES), 1)

    def chunk(c, carry):
        cs = pl.multiple_of(c * ck, ck)
        kc = k_ref[0, pl.ds(cs, ck), :]
        vc = v_ref[0, pl.ds(cs, ck), :]
        onehot = jnp.where(lane_ck == blk_of_row + c * KV_CHUNK_BLOCKS, 1.0, 0.0).astype(BF16)
        kx = jnp.concatenate([kc, onehot], axis=1)
        for h in range(HEADS_PER_PAIR):
            s = _dot_nt(qx[h], kx)
            m_old = m_scr[h][:, 0:1]
            m_new = jnp.maximum(m_old, jnp.max(s, axis=1, keepdims=True))
            p = jnp.exp(s - m_new)
            acc_scr[h] = jnp.exp(m_old - m_new) * acc_scr[h] + _dot(p.astype(BF16), v_ext(vc, h))
            m_scr[h] = jnp.broadcast_to(m_new, (tq, V7X_LANES))
        return carry

    lax.fori_loop(0, (i + KV_CHUNK_BLOCKS - 1) // KV_CHUNK_BLOCKS, chunk, 0)

    out = jnp.zeros((tq, V7X_LANES), F32)
    for h in range(HEADS_PER_PAIR):
        acc = acc_scr[h]
        den = acc[:, den_lane[h]:den_lane[h] + 1]
        out = jnp.where(head_lanes[h], acc / den, out)
    o_ref[0] = out


def _moba(q, k, v, kmean):
    B, S, _ = q.shape
    nb = S // MOBA_BLOCK
    tq = MOBA_BLOCK
    seq = pl.BlockSpec((1, S, V7X_LANES), lambda b, p, i: (b, 0, p))
    return pl.pallas_call(
        _moba_kernel,
        grid=(B, N_PAIRS, S // tq),
        in_specs=[
            pl.BlockSpec((1, tq, V7X_LANES), lambda b, p, i: (b, i, p)),
            seq,
            seq,
            pl.BlockSpec((1, nb, V7X_LANES), lambda b, p, i: (b, 0, p)),
        ],
        out_specs=pl.BlockSpec((1, tq, V7X_LANES), lambda b, p, i: (b, i, p)),
        out_shape=jax.ShapeDtypeStruct((B, S, ATTN_WIDTH), F32),
        scratch_shapes=[
            pltpu.VMEM((HEADS_PER_PAIR, tq, V7X_LANES), F32),
            pltpu.VMEM((HEADS_PER_PAIR, tq, V7X_LANES), F32),
        ],
        compiler_params=pltpu.CompilerParams(
            dimension_semantics=("arbitrary", "arbitrary", "arbitrary"),
            vmem_limit_bytes=V7X_VMEM_LIMIT_BYTES),
        name="moba",
    )(q, k, v, kmean)


def _mixer_out_kernel(x_ref, lru_ref, attn_ref, mod_ref, ga_ref, w_out_ref, g2_ref, w_up_ref, cw_ref, cb_ref,
                      w_down_ref, o_ref, up_ext):
    tm = x_ref.shape[1]

    @pl.when(pl.program_id(1) == 0)
    def _():
        up_ext[0:V7X_SUBLANES, :] = jnp.zeros((V7X_SUBLANES, 2 * D_FF), F32)

    mod = mod_ref[0]
    gate1, sh2, sc2, gate2 = mod[2:3, :], mod[3:4, :], mod[4:5, :], mod[5:6, :]

    attn = attn_ref[0]
    attn_n = (attn * _rms_scale(attn) * ga_ref[...]).astype(BF16)
    mixed = _dot(lru_ref[0], w_out_ref[0:LRU_WIDTH, :]) + _dot(attn_n, w_out_ref[LRU_WIDTH:, :])
    x1 = x_ref[0] + gate1 * mixed

    h2 = ((x1 * _rms_scale(x1) * g2_ref[...]) * (1.0 + sc2) + sh2).astype(BF16)

    def conv_cols(c0):
        cols = pl.ds(c0, FF_CHUNK)
        up_ext[V7X_SUBLANES:, cols] = _dot(h2, w_up_ref[:, cols])
        out = cb_ref[:, cols] + cw_ref[FFN_CONV - 1:FFN_CONV, cols] * up_ext[V7X_SUBLANES:, cols]
        for j in range(FFN_CONV - 1):
            back = FFN_CONV - 1 - j
            out = out + cw_ref[j:j + 1, cols] * up_ext[pl.ds(V7X_SUBLANES - back, tm), cols]
        return out

    acc = jnp.zeros((tm, D_MODEL), F32)
    for c in range(D_FF // FF_CHUNK):
        g = conv_cols(c * FF_CHUNK)
        val = conv_cols(D_FF + c * FF_CHUNK)
        act = (g * jax.nn.sigmoid(g) * val).astype(BF16)
        acc = acc + _dot(act, w_down_ref[c * FF_CHUNK:(c + 1) * FF_CHUNK, :])
    up_ext[0:V7X_SUBLANES, :] = up_ext[tm:tm + V7X_SUBLANES, :]

    o_ref[0] = x1 + gate2 * acc


def _mixer_out(x, lru, attn, mod, ga, w_out, g2, w_up, cw, cb, w_down):
    B, S, _ = x.shape
    tm = TM_OUT
    const = lambda shape: pl.BlockSpec(shape, lambda b, s: (0,) * len(shape), pipeline_mode=pl.Buffered(1))
    tok = lambda w: pl.BlockSpec((1, tm, w), lambda b, s: (b, s, 0))
    return pl.pallas_call(
        _mixer_out_kernel,
        grid=(B, S // tm),
        in_specs=[
            tok(D_MODEL),
            tok(LRU_WIDTH),
            tok(ATTN_WIDTH),
            pl.BlockSpec((1, N_MOD, D_MODEL), lambda b, s: (b, 0, 0)),
            const((1, ATTN_WIDTH)),
            const((D_MODEL, D_MODEL)),
            const((1, D_MODEL)),
            const((D_MODEL, 2 * D_FF)),
            const((FFN_CONV, 2 * D_FF)),
            const((1, 2 * D_FF)),
            const((D_FF, D_MODEL)),
        ],
        out_specs=tok(D_MODEL),
        out_shape=jax.ShapeDtypeStruct((B, S, D_MODEL), F32),
        scratch_shapes=[pltpu.VMEM((tm + V7X_SUBLANES, 2 * D_FF), F32)],
        compiler_params=pltpu.CompilerParams(
            dimension_semantics=("arbitrary", "arbitrary"), vmem_limit_bytes=V7X_VMEM_LIMIT_BYTES),
        name="mixer_out",
    )(x, lru, attn, mod, ga, w_out, g2, w_up, cw, cb, w_down)


def _block_diag(w):
    n, d, _ = w.shape
    eye = jnp.eye(n, dtype=w.dtype)
    return jnp.einsum('hij,hg->higj', w, eye).reshape(n * d, n * d)


def _layer(x, c, w_ada, b_ada, norm1_g, w_in, q_norm_g, k_norm_g, lru_conv_w, lru_conv_b, lru_wa, lru_ba,
           lru_wx, lru_bx, lru_lambda, lru_out_g, attn_out_g, w_out, norm2_g, w_up, ffn_conv_w, ffn_conv_b,
           w_down):
    B, S, _ = x.shape
    row = lambda t: t.reshape(1, -1)
    mod = _modulation(c, w_ada, row(b_ada)).reshape(B, N_MOD, D_MODEL)

    head_id = jnp.arange(ATTN_WIDTH) // HEAD_DIM
    gmat = jnp.where(head_id[:, None] == head_id[None, :], 1.0 / HEAD_DIM, 0.0).astype(BF16)
    w_gates = jnp.concatenate([_block_diag(lru_wa), _block_diag(lru_wx)], axis=1).astype(BF16)
    b_gates = jnp.concatenate([lru_ba, lru_bx]).reshape(1, -1)

    q, k, v, kmean, lru = _mixer_in(
        x, mod, row(norm1_g), w_in.astype(BF16), row(jnp.tile(q_norm_g, ATTN_HEADS)),
        row(jnp.tile(k_norm_g, ATTN_HEADS)), gmat, lru_conv_w, row(lru_conv_b), w_gates, b_gates,
        row(lru_lambda), row(lru_out_g))
    attn = _moba(q, k, v, kmean.reshape(B, S // MOBA_BLOCK, ATTN_WIDTH))
    return _mixer_out(x, lru, attn, mod, row(attn_out_g), w_out.astype(BF16), row(norm2_g), w_up.astype(BF16),
                      ffn_conv_w, row(ffn_conv_b), w_down.astype(BF16))


def kernel(x, c, w_ada, b_ada, norm1_g, w_in, q_norm_g, k_norm_g, lru_conv_w, lru_conv_b, lru_wa, lru_ba, lru_wx, lru_bx, lru_lambda, lru_out_g, attn_out_g, w_out, norm2_g, w_up, ffn_conv_w, ffn_conv_b, w_down):
    depth = w_ada.shape[0]
    for l in range(depth):
        x = _layer(x, c, w_ada[l], b_ada[l], norm1_g[l], w_in[l], q_norm_g[l], k_norm_g[l], lru_conv_w[l],
                   lru_conv_b[l], lru_wa[l], lru_ba[l], lru_wx[l], lru_bx[l], lru_lambda[l], lru_out_g[l],
                   attn_out_g[l], w_out[l], norm2_g[l], w_up[l], ffn_conv_w[l], ffn_conv_b[l], w_down[l])
    return x
```

```python
import functools
import math

import jax
import jax.numpy as jnp
from jax import lax
from jax.experimental import pallas as pl
from jax.experimental.pallas import tpu as pltpu

D_MODEL = 1024
ATTN_HEADS = 8
HEAD_DIM = 64
ATTN_WIDTH = ATTN_HEADS * HEAD_DIM
MOBA_BLOCK = 256
MOBA_TOPK = 3
LRU_WIDTH = D_MODEL - ATTN_WIDTH
LRU_HEADS = 8
LRU_CONV = 4
LRU_C = 8.0
N_IN = 3 * ATTN_WIDTH + 2 * LRU_WIDTH
D_FF = 2816
FFN_CONV = 3
N_MOD = 6
EPS = 1e-6

V7X_LANES = 128
V7X_SUBLANES = 8
V7X_VMEM_LIMIT_BYTES = 56 * 1024 * 1024

HEADS_PER_PAIR = V7X_LANES // HEAD_DIM
N_PAIRS = ATTN_HEADS // HEADS_PER_PAIR
MASK_NEG = -1e30

TM_IN = 256
TM_OUT = 256
KV_CHUNK_BLOCKS = 4
FF_CHUNK = 1408
MOD_CHUNK = 1536

BF16 = jnp.bfloat16
F32 = jnp.float32


def _dot(a, b):
    return jnp.dot(a, b, preferred_element_type=F32)


def _rms_scale(y):
    return lax.rsqrt(jnp.mean(y * y, axis=-1, keepdims=True) + EPS)


def _mod_kernel(c_ref, w_ref, b_ref, o_ref):
    o_ref[...] = _dot(c_ref[...], w_ref[...]) + b_ref[...]


def _modulation(c, w_ada, b_ada):
    B = c.shape[0]
    n = w_ada.shape[1]
    return pl.pallas_call(
        _mod_kernel,
        grid=(n // MOD_CHUNK,),
        in_specs=[
            pl.BlockSpec((B, D_MODEL), lambda j: (0, 0)),
            pl.BlockSpec((D_MODEL, MOD_CHUNK), lambda j: (0, j)),
            pl.BlockSpec((1, MOD_CHUNK), lambda j: (0, j)),
        ],
        out_specs=pl.BlockSpec((B, MOD_CHUNK), lambda j: (0, j)),
        out_shape=jax.ShapeDtypeStruct((B, n), F32),
        compiler_params=pltpu.CompilerParams(
            dimension_semantics=("arbitrary",), vmem_limit_bytes=V7X_VMEM_LIMIT_BYTES),
        name="modulation",
    )(c, w_ada, b_ada)


def _head_rmsnorm(t, gmat, gain):
    sq = t * t
    hi = sq.astype(BF16)
    lo = (sq - hi.astype(F32)).astype(BF16)
    ms = _dot(hi, gmat) + _dot(lo, gmat)
    return t * lax.rsqrt(ms + EPS) * gain


def _shift_rows(t, d, fill):
    tm = t.shape[0]
    if d % V7X_SUBLANES == 0:
        pad = jnp.full((d, t.shape[1]), fill, t.dtype)
        return jnp.concatenate([pad, t[: tm - d]], axis=0)
    row = lax.broadcasted_iota(jnp.int32, t.shape, 0)
    return jnp.where(row >= d, pltpu.roll(t, d, 0), fill)


def _linear_scan(a, u):
    tm = a.shape[0]
    d = 1
    while d < tm:
        u = a * _shift_rows(u, d, 0.0) + u
        a = a * _shift_rows(a, d, 1.0)
        d *= 2
    return a, u


def _gelu_tanh(t):
    return 0.5 * t * (1.0 + jnp.tanh(math.sqrt(2.0 / math.pi) * (t + 0.044715 * (t * t * t))))


def _mixer_in_kernel(x_ref, mod_ref, g1_ref, w_in_ref, gq_ref, gk_ref, gmat_ref, cw_ref, cb_ref,
                     wg_ref, bg_ref, lam_ref, go_ref,
                     q_ref, k_ref, v_ref, km_ref, lru_ref,
                     xr_ext, h_carry):
    tm = x_ref.shape[1]
    first = pl.program_id(1) == 0

    @pl.when(first)
    def _():
        xr_ext[0:V7X_SUBLANES, :] = jnp.zeros((V7X_SUBLANES, LRU_WIDTH), F32)
        h_carry[...] = jnp.zeros_like(h_carry)

    x = x_ref[0]
    mod = mod_ref[0]
    sh1 = mod[0:1, :]
    sc1 = mod[1:2, :]
    h = (x * _rms_scale(x) * g1_ref[...]) * (1.0 + sc1) + sh1
    hb = h.astype(BF16)

    def proj(j):
        return _dot(hb, w_in_ref[:, j * ATTN_WIDTH:(j + 1) * ATTN_WIDTH])

    gmat = gmat_ref[...]
    qn = _head_rmsnorm(proj(0), gmat, gq_ref[...])
    q_ref[0] = (qn * (HEAD_DIM ** -0.5)).T.astype(BF16)
    kn = _head_rmsnorm(proj(1), gmat, gk_ref[...])
    k_ref[0] = kn.astype(BF16)
    for blk in range(tm // MOBA_BLOCK):
        km_ref[0, blk] = jnp.mean(kn[blk * MOBA_BLOCK:(blk + 1) * MOBA_BLOCK], axis=0, keepdims=True)
    v_ref[0] = proj(2).T.astype(BF16)

    xr_ext[V7X_SUBLANES:, :] = proj(3)
    xc = cb_ref[...] + cw_ref[LRU_CONV - 1:LRU_CONV, :] * xr_ext[V7X_SUBLANES:, :]
    for j in range(LRU_CONV - 1):
        back = LRU_CONV - 1 - j
        xc = xc + cw_ref[j:j + 1, :] * xr_ext[pl.ds(V7X_SUBLANES - back, tm), :]
    xr_ext[0:V7X_SUBLANES, :] = xr_ext[tm:tm + V7X_SUBLANES, :]

    gates = _dot(xc.astype(BF16), wg_ref[...]) + bg_ref[...]
    r = jax.nn.sigmoid(gates[:, :LRU_WIDTH])
    i = jax.nn.sigmoid(gates[:, LRU_WIDTH:])
    neg_lam = -lam_ref[...]
    softplus = jnp.maximum(neg_lam, 0.0) + jnp.log1p(jnp.exp(-jnp.abs(neg_lam)))
    log_a = (-LRU_C) * r * softplus
    a = jnp.exp(log_a)
    th = jnp.tanh(log_a)
    u = jnp.sqrt(-2.0 * th / (1.0 - th)) * (i * xc)
    a_run, h_loc = _linear_scan(a, u)
    hs = h_loc + a_run * h_carry[0:1, :]
    h_carry[0:1, :] = hs[tm - 1:tm, :]

    y = hs * _gelu_tanh(proj(4))
    lru_ref[0] = (y * _rms_scale(y) * go_ref[...]).astype(BF16)


def _mixer_in(x, mod, g1, w_in, gq, gk, gmat, cw, cb, wg, bg, lam, go):
    B, S, _ = x.shape
    tm = TM_IN
    nb = S // MOBA_BLOCK
    const = lambda shape: pl.BlockSpec(shape, lambda b, s: (0,) * len(shape))
    tok = lambda w: pl.BlockSpec((1, tm, w), lambda b, s: (b, s, 0))
    feat_major = pl.BlockSpec((1, ATTN_WIDTH, tm), lambda b, s: (b, 0, s))
    return pl.pallas_call(
        _mixer_in_kernel,
        grid=(B, S // tm),
        in_specs=[
            tok(D_MODEL),
            pl.BlockSpec((1, N_MOD, D_MODEL), lambda b, s: (b, 0, 0)),
            const((1, D_MODEL)),
            const((D_MODEL, N_IN)),
            const((1, ATTN_WIDTH)),
            const((1, ATTN_WIDTH)),
            const((ATTN_WIDTH, ATTN_WIDTH)),
            const((LRU_CONV, LRU_WIDTH)),
            const((1, LRU_WIDTH)),
            const((LRU_WIDTH, 2 * LRU_WIDTH)),
            const((1, 2 * LRU_WIDTH)),
            const((1, LRU_WIDTH)),
            const((1, LRU_WIDTH)),
        ],
        out_specs=[
            feat_major,
            tok(ATTN_WIDTH),
            feat_major,
            pl.BlockSpec((1, tm // MOBA_BLOCK, 1, ATTN_WIDTH), lambda b, s: (b, s, 0, 0)),
            tok(LRU_WIDTH),
        ],
        out_shape=[
            jax.ShapeDtypeStruct((B, ATTN_WIDTH, S), BF16),
            jax.ShapeDtypeStruct((B, S, ATTN_WIDTH), BF16),
            jax.ShapeDtypeStruct((B, ATTN_WIDTH, S), BF16),
            jax.ShapeDtypeStruct((B, nb, 1, ATTN_WIDTH), F32),
            jax.ShapeDtypeStruct((B, S, LRU_WIDTH), BF16),
        ],
        scratch_shapes=[
            pltpu.VMEM((tm + V7X_SUBLANES, LRU_WIDTH), F32),
            pltpu.VMEM((V7X_SUBLANES, LRU_WIDTH), F32),
        ],
        compiler_params=pltpu.CompilerParams(
            dimension_semantics=("arbitrary", "arbitrary"), vmem_limit_bytes=V7X_VMEM_LIMIT_BYTES),
        name="mixer_in",
    )(x, mod, g1, w_in, gq, gk, gmat, cw, cb, wg, bg, lam, go)


def _select_bias(gate, n_past):
    nb = gate.shape[0]
    blk = lax.broadcasted_iota(jnp.int32, gate.shape, 0)
    neg_inf = jnp.float32(-jnp.inf)
    g = jnp.where(blk < n_past, gate, neg_inf)
    bias = jnp.full(gate.shape, MASK_NEG, F32)
    for _ in range(MOBA_TOPK):
        mx = jnp.max(g, axis=0, keepdims=True)
        cand = jnp.where((g == mx) & (mx > neg_inf), blk, nb)
        pick = blk == jnp.min(cand, axis=0, keepdims=True)
        bias = jnp.where(pick, 0.0, bias)
        g = jnp.where(pick, neg_inf, g)
    return bias


def _moba_kernel(q_ref, k_ref, v_ref, km_ref, o_ref):
    tq = q_ref.shape[2]
    nb = km_ref.shape[1]
    i = pl.program_id(2)
    feat = lax.broadcasted_iota(jnp.int32, (V7X_LANES, 1), 0)
    q_t = q_ref[0]
    km = km_ref[0].astype(BF16)

    def v_ext(h, start, n):
        vh = v_ref[0, h * HEAD_DIM:(h + 1) * HEAD_DIM, pl.ds(start, n)]
        return jnp.concatenate([vh, jnp.ones((2 * V7X_SUBLANES, n), BF16)], axis=0)

    start = pl.multiple_of(i * MOBA_BLOCK, MOBA_BLOCK)
    kd = k_ref[0, pl.ds(start, MOBA_BLOCK), :]
    key_pos = lax.broadcasted_iota(jnp.int32, (MOBA_BLOCK, tq), 0)
    qry_pos = lax.broadcasted_iota(jnp.int32, (MOBA_BLOCK, tq), 1)
    feat_pad = jnp.zeros((V7X_LANES - nb, tq), BF16)

    qx, carry = [], []
    for h in range(HEADS_PER_PAIR):
        qh = jnp.where((feat >= h * HEAD_DIM) & (feat < (h + 1) * HEAD_DIM), q_t, jnp.zeros_like(q_t))
        bias = _select_bias(_dot(km, qh), i).astype(BF16)
        qx.append(jnp.concatenate([qh, bias, feat_pad], axis=0))
        s = jnp.where(key_pos <= qry_pos, _dot(kd, qh), MASK_NEG)
        m = jnp.max(s, axis=0, keepdims=True)
        p = jnp.exp(s - m).astype(BF16)
        carry += [m, _dot(v_ext(h, start, MOBA_BLOCK), p)]
    qx = jnp.concatenate(qx, axis=1)

    ck = KV_CHUNK_BLOCKS * MOBA_BLOCK
    blk_of_key = lax.broadcasted_iota(jnp.int32, (ck, V7X_LANES), 0) // MOBA_BLOCK
    lane_ck = lax.broadcasted_iota(jnp.int32, (ck, V7X_LANES), 1)

    def chunk(c, carry):
        cs = pl.multiple_of(c * ck, ck)
        onehot = jnp.where(lane_ck == blk_of_key + c * KV_CHUNK_BLOCKS, 1.0, 0.0).astype(BF16)
        kx = jnp.concatenate([k_ref[0, pl.ds(cs, ck), :], onehot], axis=1)
        s_all = _dot(kx, qx)
        out = []
        for h in range(HEADS_PER_PAIR):
            m_old, acc = carry[2 * h], carry[2 * h + 1]
            s = s_all[:, h * tq:(h + 1) * tq]
            m_new = jnp.maximum(m_old, jnp.max(s, axis=0, keepdims=True))
            p = jnp.exp(s - m_new).astype(BF16)
            out += [m_new, jnp.exp(m_old - m_new) * acc + _dot(v_ext(h, cs, ck), p)]
        return tuple(out)

    carry = lax.fori_loop(0, (i + KV_CHUNK_BLOCKS - 1) // KV_CHUNK_BLOCKS, chunk, tuple(carry))

    heads = []
    for h in range(HEADS_PER_PAIR):
        acc = carry[2 * h + 1]
        heads.append(acc[0:HEAD_DIM] / acc[HEAD_DIM:HEAD_DIM + 1])
    o_ref[0] = jnp.concatenate(heads, axis=0).T


def _moba(q, k, v, kmean):
    B, S, _ = k.shape
    nb = S // MOBA_BLOCK
    tq = MOBA_BLOCK
    return pl.pallas_call(
        _moba_kernel,
        grid=(B, N_PAIRS, S // tq),
        in_specs=[
            pl.BlockSpec((1, V7X_LANES, tq), lambda b, p, i: (b, p, i)),
            pl.BlockSpec((1, S, V7X_LANES), lambda b, p, i: (b, 0, p)),
            pl.BlockSpec((1, V7X_LANES, S), lambda b, p, i: (b, p, 0)),
            pl.BlockSpec((1, nb, V7X_LANES), lambda b, p, i: (b, 0, p)),
        ],
        out_specs=pl.BlockSpec((1, tq, V7X_LANES), lambda b, p, i: (b, i, p)),
        out_shape=jax.ShapeDtypeStruct((B, S, ATTN_WIDTH), F32),
        compiler_params=pltpu.CompilerParams(
            dimension_semantics=("arbitrary", "arbitrary", "arbitrary"),
            vmem_limit_bytes=V7X_VMEM_LIMIT_BYTES),
        name="moba",
    )(q, k, v, kmean)


def _mixer_out_kernel(x_ref, lru_ref, attn_ref, mod_ref, ga_ref, w_out_ref, g2_ref, w_up_ref, cw_ref, cb_ref,
                      w_down_ref, o_ref, up_ext):
    tm = x_ref.shape[1]

    @pl.when(pl.program_id(1) == 0)
    def _():
        up_ext[0:V7X_SUBLANES, :] = jnp.zeros((V7X_SUBLANES, 2 * D_FF), F32)

    mod = mod_ref[0]
    gate1, sh2, sc2, gate2 = mod[2:3, :], mod[3:4, :], mod[4:5, :], mod[5:6, :]

    attn = attn_ref[0]
    attn_n = (attn * _rms_scale(attn) * ga_ref[...]).astype(BF16)
    mixed = _dot(lru_ref[0], w_out_ref[0:LRU_WIDTH, :]) + _dot(attn_n, w_out_ref[LRU_WIDTH:, :])
    x1 = x_ref[0] + gate1 * mixed

    h2 = ((x1 * _rms_scale(x1) * g2_ref[...]) * (1.0 + sc2) + sh2).astype(BF16)

    def conv_cols(c0):
        cols = pl.ds(c0, FF_CHUNK)
        up_ext[V7X_SUBLANES:, cols] = _dot(h2, w_up_ref[:, cols])
        out = cb_ref[:, cols] + cw_ref[FFN_CONV - 1:FFN_CONV, cols] * up_ext[V7X_SUBLANES:, cols]
        for j in range(FFN_CONV - 1):
            back = FFN_CONV - 1 - j
            out = out + cw_ref[j:j + 1, cols] * up_ext[pl.ds(V7X_SUBLANES - back, tm), cols]
        return out

    acc = jnp.zeros((tm, D_MODEL), F32)
    for c in range(D_FF // FF_CHUNK):
        g = conv_cols(c * FF_CHUNK)
        val = conv_cols(D_FF + c * FF_CHUNK)
        act = (g * jax.nn.sigmoid(g) * val).astype(BF16)
        acc = acc + _dot(act, w_down_ref[c * FF_CHUNK:(c + 1) * FF_CHUNK, :])
    up_ext[0:V7X_SUBLANES, :] = up_ext[tm:tm + V7X_SUBLANES, :]

    o_ref[0] = x1 + gate2 * acc


def _mixer_out(x, lru, attn, mod, ga, w_out, g2, w_up, cw, cb, w_down):
    B, S, _ = x.shape
    tm = TM_OUT
    const = lambda shape: pl.BlockSpec(shape, lambda b, s: (0,) * len(shape), pipeline_mode=pl.Buffered(1))
    tok = lambda w: pl.BlockSpec((1, tm, w), lambda b, s: (b, s, 0))
    return pl.pallas_call(
        _mixer_out_kernel,
        grid=(B, S // tm),
        in_specs=[
            tok(D_MODEL),
            tok(LRU_WIDTH),
            tok(ATTN_WIDTH),
            pl.BlockSpec((1, N_MOD, D_MODEL), lambda b, s: (b, 0, 0)),
            const((1, ATTN_WIDTH)),
            const((D_MODEL, D_MODEL)),
            const((1, D_MODEL)),
            const((D_MODEL, 2 * D_FF)),
            const((FFN_CONV, 2 * D_FF)),
            const((1, 2 * D_FF)),
            const((D_FF, D_MODEL)),
        ],
        out_specs=tok(D_MODEL),
        out_shape=jax.ShapeDtypeStruct((B, S, D_MODEL), F32),
        scratch_shapes=[pltpu.VMEM((tm + V7X_SUBLANES, 2 * D_FF), F32)],
        compiler_params=pltpu.CompilerParams(
            dimension_semantics=("arbitrary", "arbitrary"), vmem_limit_bytes=V7X_VMEM_LIMIT_BYTES),
        name="mixer_out",
    )(x, lru, attn, mod, ga, w_out, g2, w_up, cw, cb, w_down)


def _block_diag(w):
    n, d, _ = w.shape
    eye = jnp.eye(n, dtype=w.dtype)
    return jnp.einsum('hij,hg->higj', w, eye).reshape(n * d, n * d)


def _layer(x, c, w_ada, b_ada, norm1_g, w_in, q_norm_g, k_norm_g, lru_conv_w, lru_conv_b, lru_wa, lru_ba,
           lru_wx, lru_bx, lru_lambda, lru_out_g, attn_out_g, w_out, norm2_g, w_up, ffn_conv_w, ffn_conv_b,
           w_down):
    B, S, _ = x.shape
    row = lambda t: t.reshape(1, -1)
    mod = _modulation(c, w_ada, row(b_ada)).reshape(B, N_MOD, D_MODEL)

    head_id = jnp.arange(ATTN_WIDTH) // HEAD_DIM
    gmat = jnp.where(head_id[:, None] == head_id[None, :], 1.0 / HEAD_DIM, 0.0).astype(BF16)
    w_gates = jnp.concatenate([_block_diag(lru_wa), _block_diag(lru_wx)], axis=1).astype(BF16)
    b_gates = jnp.concatenate([lru_ba, lru_bx]).reshape(1, -1)

    q, k, v, kmean, lru = _mixer_in(
        x, mod, row(norm1_g), w_in.astype(BF16), row(jnp.tile(q_norm_g, ATTN_HEADS)),
        row(jnp.tile(k_norm_g, ATTN_HEADS)), gmat, lru_conv_w, row(lru_conv_b), w_gates, b_gates,
        row(lru_lambda), row(lru_out_g))
    attn = _moba(q, k, v, kmean.reshape(B, S // MOBA_BLOCK, ATTN_WIDTH))
    return _mixer_out(x, lru, attn, mod, row(attn_out_g), w_out.astype(BF16), row(norm2_g), w_up.astype(BF16),
                      ffn_conv_w, row(ffn_conv_b), w_down.astype(BF16))


def kernel(x, c, w_ada, b_ada, norm1_g, w_in, q_norm_g, k_norm_g, lru_conv_w, lru_conv_b, lru_wa, lru_ba, lru_wx, lru_bx, lru_lambda, lru_out_g, attn_out_g, w_out, norm2_g, w_up, ffn_conv_w, ffn_conv_b, w_down):
    depth = w_ada.shape[0]
    for l in range(depth):
        x = _layer(x, c, w_ada[l], b_ada[l], norm1_g[l], w_in[l], q_norm_g[l], k_norm_g[l], lru_conv_w[l],
                   lru_conv_b[l], lru_wa[l], lru_ba[l], lru_wx[l], lru_bx[l], lru_lambda[l], lru_out_g[l],
                   attn_out_g[l], w_out[l], norm2_g[l], w_up[l], ffn_conv_w[l], ffn_conv_b[l], w_down[l])
    return x
```

```python
import functools
import math

import jax
import jax.numpy as jnp
from jax import lax
from jax.experimental import pallas as pl
from jax.experimental.pallas import tpu as pltpu

D_MODEL = 1024
ATTN_HEADS = 8
HEAD_DIM = 64
ATTN_WIDTH = ATTN_HEADS * HEAD_DIM
MOBA_BLOCK = 256
MOBA_TOPK = 3
LRU_WIDTH = D_MODEL - ATTN_WIDTH
LRU_HEADS = 8
LRU_CONV = 4
LRU_C = 8.0
N_IN = 3 * ATTN_WIDTH + 2 * LRU_WIDTH
D_FF = 2816
FFN_CONV = 3
N_MOD = 6
EPS = 1e-6

V7X_LANES = 128
V7X_SUBLANES = 8
V7X_VMEM_LIMIT_BYTES = 56 * 1024 * 1024

HEADS_PER_PAIR = V7X_LANES // HEAD_DIM
N_PAIRS = ATTN_HEADS // HEADS_PER_PAIR
MASK_NEG = -1e30

TM_IN = 256
TM_OUT = 256
KV_CHUNK_BLOCKS = 4
FF_CHUNK = 1408
MOD_CHUNK = 1536

BF16 = jnp.bfloat16
F32 = jnp.float32


def _dot(a, b):
    return jnp.dot(a, b, preferred_element_type=F32)


def _rms_scale(y):
    return lax.rsqrt(jnp.mean(y * y, axis=-1, keepdims=True) + EPS)


def _mod_kernel(c_ref, w_ref, b_ref, o_ref):
    o_ref[...] = _dot(c_ref[...], w_ref[...]) + b_ref[...]


def _modulation(c, w_ada, b_ada):
    B = c.shape[0]
    n = w_ada.shape[1]
    return pl.pallas_call(
        _mod_kernel,
        grid=(n // MOD_CHUNK,),
        in_specs=[
            pl.BlockSpec((B, D_MODEL), lambda j: (0, 0)),
            pl.BlockSpec((D_MODEL, MOD_CHUNK), lambda j: (0, j)),
            pl.BlockSpec((1, MOD_CHUNK), lambda j: (0, j)),
        ],
        out_specs=pl.BlockSpec((B, MOD_CHUNK), lambda j: (0, j)),
        out_shape=jax.ShapeDtypeStruct((B, n), F32),
        compiler_params=pltpu.CompilerParams(
            dimension_semantics=("arbitrary",), vmem_limit_bytes=V7X_VMEM_LIMIT_BYTES),
        name="modulation",
    )(c, w_ada, b_ada)


def _head_rmsnorm(t, gmat, gain):
    sq = t * t
    hi = sq.astype(BF16)
    lo = (sq - hi.astype(F32)).astype(BF16)
    ms = _dot(hi, gmat) + _dot(lo, gmat)
    return t * lax.rsqrt(ms + EPS) * gain


def _shift_rows(t, d, fill):
    tm = t.shape[0]
    if d % V7X_SUBLANES == 0:
        pad = jnp.full((d, t.shape[1]), fill, t.dtype)
        return jnp.concatenate([pad, t[: tm - d]], axis=0)
    row = lax.broadcasted_iota(jnp.int32, t.shape, 0)
    return jnp.where(row >= d, pltpu.roll(t, d, 0), fill)


def _linear_scan(a, u):
    tm = a.shape[0]
    d = 1
    while d < tm:
        u = a * _shift_rows(u, d, 0.0) + u
        a = a * _shift_rows(a, d, 1.0)
        d *= 2
    return a, u


def _gelu_tanh(t):
    return 0.5 * t * (1.0 + jnp.tanh(math.sqrt(2.0 / math.pi) * (t + 0.044715 * (t * t * t))))


def _mixer_in_kernel(x_ref, mod_ref, g1_ref, w_in_ref, gq_ref, gk_ref, gmat_ref, cw_ref, cb_ref,
                     wg_ref, bg_ref, lam_ref, go_ref,
                     q_ref, k_ref, v_ref, km_ref, lru_ref,
                     xr_ext, h_carry):
    tm = x_ref.shape[1]
    first = pl.program_id(1) == 0

    @pl.when(first)
    def _():
        xr_ext[0:V7X_SUBLANES, :] = jnp.zeros((V7X_SUBLANES, LRU_WIDTH), F32)
        h_carry[...] = jnp.zeros_like(h_carry)

    x = x_ref[0]
    mod = mod_ref[0]
    sh1 = mod[0:1, :]
    sc1 = mod[1:2, :]
    h = (x * _rms_scale(x) * g1_ref[...]) * (1.0 + sc1) + sh1
    hb = h.astype(BF16)

    def proj(j):
        return _dot(hb, w_in_ref[:, j * ATTN_WIDTH:(j + 1) * ATTN_WIDTH])

    gmat = gmat_ref[...]
    qn = _head_rmsnorm(proj(0), gmat, gq_ref[...])
    q_ref[0] = (qn * (HEAD_DIM ** -0.5)).T.astype(BF16)
    kn = _head_rmsnorm(proj(1), gmat, gk_ref[...])
    k_ref[0] = kn.astype(BF16)
    for blk in range(tm // MOBA_BLOCK):
        km_ref[0, blk] = jnp.mean(kn[blk * MOBA_BLOCK:(blk + 1) * MOBA_BLOCK], axis=0, keepdims=True)
    v_ref[0] = proj(2).T.astype(BF16)

    xr_ext[V7X_SUBLANES:, :] = proj(3)
    xc = cb_ref[...] + cw_ref[LRU_CONV - 1:LRU_CONV, :] * xr_ext[V7X_SUBLANES:, :]
    for j in range(LRU_CONV - 1):
        back = LRU_CONV - 1 - j
        xc = xc + cw_ref[j:j + 1, :] * xr_ext[pl.ds(V7X_SUBLANES - back, tm), :]
    xr_ext[0:V7X_SUBLANES, :] = xr_ext[tm:tm + V7X_SUBLANES, :]

    gates = _dot(xc.astype(BF16), wg_ref[...]) + bg_ref[...]
    r = jax.nn.sigmoid(gates[:, :LRU_WIDTH])
    i = jax.nn.sigmoid(gates[:, LRU_WIDTH:])
    neg_lam = -lam_ref[...]
    softplus = jnp.maximum(neg_lam, 0.0) + jnp.log1p(jnp.exp(-jnp.abs(neg_lam)))
    log_a = (-LRU_C) * r * softplus
    a = jnp.exp(log_a)
    th = jnp.tanh(log_a)
    u = jnp.sqrt(-2.0 * th / (1.0 - th)) * (i * xc)
    a_run, h_loc = _linear_scan(a, u)
    hs = h_loc + a_run * h_carry[0:1, :]
    h_carry[0:1, :] = hs[tm - 1:tm, :]

    y = hs * _gelu_tanh(proj(4))
    lru_ref[0] = (y * _rms_scale(y) * go_ref[...]).astype(BF16)


def _mixer_in(x, mod, g1, w_in, gq, gk, gmat, cw, cb, wg, bg, lam, go):
    B, S, _ = x.shape
    tm = TM_IN
    nb = S // MOBA_BLOCK
    const = lambda shape: pl.BlockSpec(shape, lambda b, s: (0,) * len(shape))
    tok = lambda w: pl.BlockSpec((1, tm, w), lambda b, s: (b, s, 0))
    feat_major = pl.BlockSpec((1, ATTN_WIDTH, tm), lambda b, s: (b, 0, s))
    return pl.pallas_call(
        _mixer_in_kernel,
        grid=(B, S // tm),
        in_specs=[
            tok(D_MODEL),
            pl.BlockSpec((1, N_MOD, D_MODEL), lambda b, s: (b, 0, 0)),
            const((1, D_MODEL)),
            const((D_MODEL, N_IN)),
            const((1, ATTN_WIDTH)),
            const((1, ATTN_WIDTH)),
            const((ATTN_WIDTH, ATTN_WIDTH)),
            const((LRU_CONV, LRU_WIDTH)),
            const((1, LRU_WIDTH)),
            const((LRU_WIDTH, 2 * LRU_WIDTH)),
            const((1, 2 * LRU_WIDTH)),
            const((1, LRU_WIDTH)),
            const((1, LRU_WIDTH)),
        ],
        out_specs=[
            feat_major,
            tok(ATTN_WIDTH),
            feat_major,
            pl.BlockSpec((1, tm // MOBA_BLOCK, 1, ATTN_WIDTH), lambda b, s: (b, s, 0, 0)),
            tok(LRU_WIDTH),
        ],
        out_shape=[
            jax.ShapeDtypeStruct((B, ATTN_WIDTH, S), BF16),
            jax.ShapeDtypeStruct((B, S, ATTN_WIDTH), BF16),
            jax.ShapeDtypeStruct((B, ATTN_WIDTH, S), BF16),
            jax.ShapeDtypeStruct((B, nb, 1, ATTN_WIDTH), F32),
            jax.ShapeDtypeStruct((B, S, LRU_WIDTH), BF16),
        ],
        scratch_shapes=[
            pltpu.VMEM((tm + V7X_SUBLANES, LRU_WIDTH), F32),
            pltpu.VMEM((V7X_SUBLANES, LRU_WIDTH), F32),
        ],
        compiler_params=pltpu.CompilerParams(
            dimension_semantics=("arbitrary", "arbitrary"), vmem_limit_bytes=V7X_VMEM_LIMIT_BYTES),
        name="mixer_in",
    )(x, mod, g1, w_in, gq, gk, gmat, cw, cb, wg, bg, lam, go)


def _select_bias(gate, n_past):
    nb = gate.shape[0]
    blk = lax.broadcasted_iota(jnp.int32, gate.shape, 0)
    neg_inf = jnp.float32(-jnp.inf)
    g = jnp.where(blk < n_past, gate, neg_inf)
    bias = jnp.full(gate.shape, MASK_NEG, F32)
    for _ in range(MOBA_TOPK):
        mx = jnp.max(g, axis=0, keepdims=True)
        cand = jnp.where((g == mx) & (mx > neg_inf), blk, nb)
        pick = blk == jnp.min(cand, axis=0, keepdims=True)
        bias = jnp.where(pick, 0.0, bias)
        g = jnp.where(pick, neg_inf, g)
    return bias


def _moba_kernel(q_ref, k_ref, v_ref, km_ref, blk_ref, o_ref, s_even, s_odd):
    tq = q_ref.shape[2]
    nb = km_ref.shape[1]
    i = pl.program_id(2)
    feat = lax.broadcasted_iota(jnp.int32, (V7X_LANES, 1), 0)
    q_t = q_ref[0]
    km = km_ref[0].astype(BF16)

    def v_ext(h, start, n):
        vh = v_ref[0, h * HEAD_DIM:(h + 1) * HEAD_DIM, pl.ds(start, n)]
        return jnp.concatenate([vh, jnp.ones((2 * V7X_SUBLANES, n), BF16)], axis=0)

    start = pl.multiple_of(i * MOBA_BLOCK, MOBA_BLOCK)
    kd = k_ref[0, pl.ds(start, MOBA_BLOCK), :]
    key_pos = lax.broadcasted_iota(jnp.int32, (MOBA_BLOCK, tq), 0)
    qry_pos = lax.broadcasted_iota(jnp.int32, (MOBA_BLOCK, tq), 1)
    feat_pad = jnp.zeros((V7X_LANES - nb, tq), BF16)

    qx, carry = [], []
    for h in range(HEADS_PER_PAIR):
        qh = jnp.where((feat >= h * HEAD_DIM) & (feat < (h + 1) * HEAD_DIM), q_t, jnp.zeros_like(q_t))
        bias = _select_bias(_dot(km, qh), i).astype(BF16)
        qx.append(jnp.concatenate([qh, bias, feat_pad], axis=0))
        s = jnp.where(key_pos <= qry_pos, _dot(kd, qh), MASK_NEG)
        m = jnp.max(s, axis=0, keepdims=True)
        p = jnp.exp(s - m).astype(BF16)
        carry += [m, _dot(v_ext(h, start, MOBA_BLOCK), p)]
    qx = jnp.concatenate(qx, axis=1)

    ck = KV_CHUNK_BLOCKS * MOBA_BLOCK
    n_chunks = (i + KV_CHUNK_BLOCKS - 1) // KV_CHUNK_BLOCKS

    def scores(c):
        cs = pl.multiple_of(c * ck, ck)
        kx = jnp.concatenate([k_ref[0, pl.ds(cs, ck), :], blk_ref[pl.ds(cs, ck), :]], axis=1)
        return _dot(kx, qx)

    def softmax_step(c, s_ref, carry):
        cs = pl.multiple_of(c * ck, ck)
        out = []
        for h in range(HEADS_PER_PAIR):
            m_old, acc = carry[2 * h], carry[2 * h + 1]
            s = s_ref[:, h * tq:(h + 1) * tq]
            m_new = jnp.maximum(m_old, jnp.max(s, axis=0, keepdims=True))
            p = jnp.exp(s - m_new).astype(BF16)
            out += [m_new, jnp.exp(m_old - m_new) * acc + _dot(v_ext(h, cs, ck), p)]
        return tuple(out)

    s_even[...] = scores(0)

    def chunk(c, carry):
        nxt = jnp.minimum(c + 1, jnp.maximum(n_chunks - 1, 0))

        def step(cur_ref, nxt_ref):
            nxt_ref[...] = scores(nxt)
            return softmax_step(c, cur_ref, carry)

        return lax.cond(c % 2 == 0, lambda: step(s_even, s_odd), lambda: step(s_odd, s_even))

    carry = lax.fori_loop(0, n_chunks, chunk, tuple(carry))

    heads = []
    for h in range(HEADS_PER_PAIR):
        acc = carry[2 * h + 1]
        heads.append(acc[0:HEAD_DIM] / acc[HEAD_DIM:HEAD_DIM + 1])
    o_ref[0] = jnp.concatenate(heads, axis=0).T


def _moba(q, k, v, kmean):
    B, S, _ = k.shape
    nb = S // MOBA_BLOCK
    tq = MOBA_BLOCK
    ck = KV_CHUNK_BLOCKS * MOBA_BLOCK
    block_onehot = (jnp.arange(S)[:, None] // MOBA_BLOCK == jnp.arange(V7X_LANES)[None, :]).astype(BF16)
    return pl.pallas_call(
        _moba_kernel,
        grid=(B, N_PAIRS, S // tq),
        in_specs=[
            pl.BlockSpec((1, V7X_LANES, tq), lambda b, p, i: (b, p, i)),
            pl.BlockSpec((1, S, V7X_LANES), lambda b, p, i: (b, 0, p)),
            pl.BlockSpec((1, V7X_LANES, S), lambda b, p, i: (b, p, 0)),
            pl.BlockSpec((1, nb, V7X_LANES), lambda b, p, i: (b, 0, p)),
            pl.BlockSpec((S, V7X_LANES), lambda b, p, i: (0, 0), pipeline_mode=pl.Buffered(1)),
        ],
        out_specs=pl.BlockSpec((1, tq, V7X_LANES), lambda b, p, i: (b, i, p)),
        out_shape=jax.ShapeDtypeStruct((B, S, ATTN_WIDTH), F32),
        scratch_shapes=[
            pltpu.VMEM((ck, HEADS_PER_PAIR * tq), F32),
            pltpu.VMEM((ck, HEADS_PER_PAIR * tq), F32),
        ],
        compiler_params=pltpu.CompilerParams(
            dimension_semantics=("arbitrary", "arbitrary", "arbitrary"),
            vmem_limit_bytes=V7X_VMEM_LIMIT_BYTES),
        name="moba",
    )(q, k, v, kmean, block_onehot)


def _mixer_out_kernel(x_ref, lru_ref, attn_ref, mod_ref, ga_ref, w_out_ref, g2_ref, w_up_ref, cw_ref, cb_ref,
                      w_down_ref, o_ref, up_ext):
    tm = x_ref.shape[1]

    @pl.when(pl.program_id(1) == 0)
    def _():
        up_ext[0:V7X_SUBLANES, :] = jnp.zeros((V7X_SUBLANES, 2 * D_FF), F32)

    mod = mod_ref[0]
    gate1, sh2, sc2, gate2 = mod[2:3, :], mod[3:4, :], mod[4:5, :], mod[5:6, :]

    attn = attn_ref[0]
    attn_n = (attn * _rms_scale(attn) * ga_ref[...]).astype(BF16)
    mixed = _dot(lru_ref[0], w_out_ref[0:LRU_WIDTH, :]) + _dot(attn_n, w_out_ref[LRU_WIDTH:, :])
    x1 = x_ref[0] + gate1 * mixed

    h2 = ((x1 * _rms_scale(x1) * g2_ref[...]) * (1.0 + sc2) + sh2).astype(BF16)

    def conv_cols(c0):
        cols = pl.ds(c0, FF_CHUNK)
        up_ext[V7X_SUBLANES:, cols] = _dot(h2, w_up_ref[:, cols])
        out = cb_ref[:, cols] + cw_ref[FFN_CONV - 1:FFN_CONV, cols] * up_ext[V7X_SUBLANES:, cols]
        for j in range(FFN_CONV - 1):
            back = FFN_CONV - 1 - j
            out = out + cw_ref[j:j + 1, cols] * up_ext[pl.ds(V7X_SUBLANES - back, tm), cols]
        return out

    acc = jnp.zeros((tm, D_MODEL), F32)
    for c in range(D_FF // FF_CHUNK):
        g = conv_cols(c * FF_CHUNK)
        val = conv_cols(D_FF + c * FF_CHUNK)
        act = (g * jax.nn.sigmoid(g) * val).astype(BF16)
        acc = acc + _dot(act, w_down_ref[c * FF_CHUNK:(c + 1) * FF_CHUNK, :])
    up_ext[0:V7X_SUBLANES, :] = up_ext[tm:tm + V7X_SUBLANES, :]

    o_ref[0] = x1 + gate2 * acc


def _mixer_out(x, lru, attn, mod, ga, w_out, g2, w_up, cw, cb, w_down):
    B, S, _ = x.shape
    tm = TM_OUT
    const = lambda shape: pl.BlockSpec(shape, lambda b, s: (0,) * len(shape), pipeline_mode=pl.Buffered(1))
    tok = lambda w: pl.BlockSpec((1, tm, w), lambda b, s: (b, s, 0))
    return pl.pallas_call(
        _mixer_out_kernel,
        grid=(B, S // tm),
        in_specs=[
            tok(D_MODEL),
            tok(LRU_WIDTH),
            tok(ATTN_WIDTH),
            pl.BlockSpec((1, N_MOD, D_MODEL), lambda b, s: (b, 0, 0)),
            const((1, ATTN_WIDTH)),
            const((D_MODEL, D_MODEL)),
            const((1, D_MODEL)),
            const((D_MODEL, 2 * D_FF)),
            const((FFN_CONV, 2 * D_FF)),
            const((1, 2 * D_FF)),
            const((D_FF, D_MODEL)),
        ],
        out_specs=tok(D_MODEL),
        out_shape=jax.ShapeDtypeStruct((B, S, D_MODEL), F32),
        scratch_shapes=[pltpu.VMEM((tm + V7X_SUBLANES, 2 * D_FF), F32)],
        compiler_params=pltpu.CompilerParams(
            dimension_semantics=("arbitrary", "arbitrary"), vmem_limit_bytes=V7X_VMEM_LIMIT_BYTES),
        name="mixer_out",
    )(x, lru, attn, mod, ga, w_out, g2, w_up, cw, cb, w_down)


def _block_diag(w):
    n, d, _ = w.shape
    eye = jnp.eye(n, dtype=w.dtype)
    return jnp.einsum('hij,hg->higj', w, eye).reshape(n * d, n * d)


def _layer(x, c, w_ada, b_ada, norm1_g, w_in, q_norm_g, k_norm_g, lru_conv_w, lru_conv_b, lru_wa, lru_ba,
           lru_wx, lru_bx, lru_lambda, lru_out_g, attn_out_g, w_out, norm2_g, w_up, ffn_conv_w, ffn_conv_b,
           w_down):
    B, S, _ = x.shape
    row = lambda t: t.reshape(1, -1)
    mod = _modulation(c, w_ada, row(b_ada)).reshape(B, N_MOD, D_MODEL)

    head_id = jnp.arange(ATTN_WIDTH) // HEAD_DIM
    gmat = jnp.where(head_id[:, None] == head_id[None, :], 1.0 / HEAD_DIM, 0.0).astype(BF16)
    w_gates = jnp.concatenate([_block_diag(lru_wa), _block_diag(lru_wx)], axis=1).astype(BF16)
    b_gates = jnp.concatenate([lru_ba, lru_bx]).reshape(1, -1)

    q, k, v, kmean, lru = _mixer_in(
        x, mod, row(norm1_g), w_in.astype(BF16), row(jnp.tile(q_norm_g, ATTN_HEADS)),
        row(jnp.tile(k_norm_g, ATTN_HEADS)), gmat, lru_conv_w, row(lru_conv_b), w_gates, b_gates,
        row(lru_lambda), row(lru_out_g))
    attn = _moba(q, k, v, kmean.reshape(B, S // MOBA_BLOCK, ATTN_WIDTH))
    return _mixer_out(x, lru, attn, mod, row(attn_out_g), w_out.astype(BF16), row(norm2_g), w_up.astype(BF16),
                      ffn_conv_w, row(ffn_conv_b), w_down.astype(BF16))


def kernel(x, c, w_ada, b_ada, norm1_g, w_in, q_norm_g, k_norm_g, lru_conv_w, lru_conv_b, lru_wa, lru_ba, lru_wx, lru_bx, lru_lambda, lru_out_g, attn_out_g, w_out, norm2_g, w_up, ffn_conv_w, ffn_conv_b, w_down):
    depth = w_ada.shape[0]
    for l in range(depth):
        x = _layer(x, c, w_ada[l], b_ada[l], norm1_g[l], w_in[l], q_norm_g[l], k_norm_g[l], lru_conv_w[l],
                   lru_conv_b[l], lru_wa[l], lru_ba[l], lru_wx[l], lru_bx[l], lru_lambda[l], lru_out_g[l],
                   attn_out_g[l], w_out[l], norm2_g[l], w_up[l], ffn_conv_w[l], ffn_conv_b[l], w_down[l])
    return x
```

```python
import functools
import math

import jax
import jax.numpy as jnp
from jax import lax
from jax.experimental import pallas as pl
from jax.experimental.pallas import tpu as pltpu

D_MODEL = 1024
ATTN_HEADS = 8
HEAD_DIM = 64
ATTN_WIDTH = ATTN_HEADS * HEAD_DIM
MOBA_BLOCK = 256
MOBA_TOPK = 3
LRU_WIDTH = D_MODEL - ATTN_WIDTH
LRU_HEADS = 8
LRU_CONV = 4
LRU_C = 8.0
N_IN = 3 * ATTN_WIDTH + 2 * LRU_WIDTH
D_FF = 2816
FFN_CONV = 3
N_MOD = 6
EPS = 1e-6

V7X_LANES = 128
V7X_SUBLANES = 8
V7X_VMEM_LIMIT_BYTES = 56 * 1024 * 1024

HEADS_PER_PAIR = V7X_LANES // HEAD_DIM
N_PAIRS = ATTN_HEADS // HEADS_PER_PAIR
MASK_NEG = -1e30

TM_IN = 256
TM_OUT = 256
KV_CHUNK_BLOCKS = 4
FF_CHUNK = 1408
MOD_CHUNK = 1536

BF16 = jnp.bfloat16
F32 = jnp.float32


def _dot(a, b):
    return jnp.dot(a, b, preferred_element_type=F32)


def _rms_scale(y):
    return lax.rsqrt(jnp.mean(y * y, axis=-1, keepdims=True) + EPS)


def _mod_kernel(c_ref, w_ref, b_ref, o_ref):
    o_ref[...] = _dot(c_ref[...], w_ref[...]) + b_ref[...]


def _modulation(c, w_ada, b_ada):
    B = c.shape[0]
    n = w_ada.shape[1]
    return pl.pallas_call(
        _mod_kernel,
        grid=(n // MOD_CHUNK,),
        in_specs=[
            pl.BlockSpec((B, D_MODEL), lambda j: (0, 0)),
            pl.BlockSpec((D_MODEL, MOD_CHUNK), lambda j: (0, j)),
            pl.BlockSpec((1, MOD_CHUNK), lambda j: (0, j)),
        ],
        out_specs=pl.BlockSpec((B, MOD_CHUNK), lambda j: (0, j)),
        out_shape=jax.ShapeDtypeStruct((B, n), F32),
        compiler_params=pltpu.CompilerParams(
            dimension_semantics=("arbitrary",), vmem_limit_bytes=V7X_VMEM_LIMIT_BYTES),
        name="modulation",
    )(c, w_ada, b_ada)


def _head_rmsnorm(t, gmat, gain):
    sq = t * t
    hi = sq.astype(BF16)
    lo = (sq - hi.astype(F32)).astype(BF16)
    ms = _dot(hi, gmat) + _dot(lo, gmat)
    return t * lax.rsqrt(ms + EPS) * gain


def _shift_rows(t, d, fill):
    tm = t.shape[0]
    if d % V7X_SUBLANES == 0:
        pad = jnp.full((d, t.shape[1]), fill, t.dtype)
        return jnp.concatenate([pad, t[: tm - d]], axis=0)
    row = lax.broadcasted_iota(jnp.int32, t.shape, 0)
    return jnp.where(row >= d, pltpu.roll(t, d, 0), fill)


def _linear_scan(a, u):
    tm = a.shape[0]
    d = 1
    while d < tm:
        u = a * _shift_rows(u, d, 0.0) + u
        a = a * _shift_rows(a, d, 1.0)
        d *= 2
    return a, u


def _gelu_tanh(t):
    return 0.5 * t * (1.0 + jnp.tanh(math.sqrt(2.0 / math.pi) * (t + 0.044715 * (t * t * t))))


def _mixer_in_kernel(x_ref, mod_ref, g1_ref, w_in_ref, gq_ref, gk_ref, gmat_ref, cw_ref, cb_ref,
                     wg_ref, bg_ref, lam_ref, go_ref,
                     q_ref, k_ref, v_ref, bias_ref, lru_ref,
                     xr_ext, h_carry, km_all):
    tm = x_ref.shape[1]
    blk_id = pl.program_id(1)
    first = blk_id == 0

    @pl.when(first)
    def _():
        xr_ext[0:V7X_SUBLANES, :] = jnp.zeros((V7X_SUBLANES, LRU_WIDTH), F32)
        h_carry[...] = jnp.zeros_like(h_carry)
        km_all[...] = jnp.zeros_like(km_all)

    x = x_ref[0]
    mod = mod_ref[0]
    sh1 = mod[0:1, :]
    sc1 = mod[1:2, :]
    h = (x * _rms_scale(x) * g1_ref[...]) * (1.0 + sc1) + sh1
    hb = h.astype(BF16)

    def proj(j):
        return _dot(hb, w_in_ref[:, j * ATTN_WIDTH:(j + 1) * ATTN_WIDTH])

    gmat = gmat_ref[...]
    qn = _head_rmsnorm(proj(0), gmat, gq_ref[...])
    q_t = (qn * (HEAD_DIM ** -0.5)).T.astype(BF16)
    q_ref[0] = q_t

    nb = km_all.shape[0]
    km = km_all[...].astype(BF16)
    feat = lax.broadcasted_iota(jnp.int32, (V7X_LANES, 1), 0)
    for head in range(ATTN_HEADS):
        p, h = divmod(head, HEADS_PER_PAIR)
        q_pair = q_t[p * V7X_LANES:(p + 1) * V7X_LANES]
        qh = jnp.where((feat >= h * HEAD_DIM) & (feat < (h + 1) * HEAD_DIM), q_pair, jnp.zeros_like(q_pair))
        gate = _dot(km[:, p * V7X_LANES:(p + 1) * V7X_LANES], qh)
        bias_ref[0, head * nb:(head + 1) * nb, :] = _select_bias(gate, blk_id).astype(BF16)

    kn = _head_rmsnorm(proj(1), gmat, gk_ref[...])
    k_ref[0] = kn.astype(BF16)
    km_all[pl.ds(blk_id, 1), :] = jnp.mean(kn, axis=0, keepdims=True)
    v_ref[0] = proj(2).T.astype(BF16)

    xr_ext[V7X_SUBLANES:, :] = proj(3)
    xc = cb_ref[...] + cw_ref[LRU_CONV - 1:LRU_CONV, :] * xr_ext[V7X_SUBLANES:, :]
    for j in range(LRU_CONV - 1):
        back = LRU_CONV - 1 - j
        xc = xc + cw_ref[j:j + 1, :] * xr_ext[pl.ds(V7X_SUBLANES - back, tm), :]
    xr_ext[0:V7X_SUBLANES, :] = xr_ext[tm:tm + V7X_SUBLANES, :]

    gates = _dot(xc.astype(BF16), wg_ref[...]) + bg_ref[...]
    r = jax.nn.sigmoid(gates[:, :LRU_WIDTH])
    i = jax.nn.sigmoid(gates[:, LRU_WIDTH:])
    neg_lam = -lam_ref[...]
    softplus = jnp.maximum(neg_lam, 0.0) + jnp.log1p(jnp.exp(-jnp.abs(neg_lam)))
    log_a = (-LRU_C) * r * softplus
    a = jnp.exp(log_a)
    th = jnp.tanh(log_a)
    u = jnp.sqrt(-2.0 * th / (1.0 - th)) * (i * xc)
    a_run, h_loc = _linear_scan(a, u)
    hs = h_loc + a_run * h_carry[0:1, :]
    h_carry[0:1, :] = hs[tm - 1:tm, :]

    y = hs * _gelu_tanh(proj(4))
    lru_ref[0] = (y * _rms_scale(y) * go_ref[...]).astype(BF16)


def _mixer_in(x, mod, g1, w_in, gq, gk, gmat, cw, cb, wg, bg, lam, go):
    B, S, _ = x.shape
    tm = TM_IN
    assert tm == MOBA_BLOCK, "one MoBA block per token tile: the block choice reads the means of earlier tiles"
    nb = S // MOBA_BLOCK
    const = lambda shape: pl.BlockSpec(shape, lambda b, s: (0,) * len(shape))
    tok = lambda w: pl.BlockSpec((1, tm, w), lambda b, s: (b, s, 0))
    feat_major = pl.BlockSpec((1, ATTN_WIDTH, tm), lambda b, s: (b, 0, s))
    return pl.pallas_call(
        _mixer_in_kernel,
        grid=(B, S // tm),
        in_specs=[
            tok(D_MODEL),
            pl.BlockSpec((1, N_MOD, D_MODEL), lambda b, s: (b, 0, 0)),
            const((1, D_MODEL)),
            const((D_MODEL, N_IN)),
            const((1, ATTN_WIDTH)),
            const((1, ATTN_WIDTH)),
            const((ATTN_WIDTH, ATTN_WIDTH)),
            const((LRU_CONV, LRU_WIDTH)),
            const((1, LRU_WIDTH)),
            const((LRU_WIDTH, 2 * LRU_WIDTH)),
            const((1, 2 * LRU_WIDTH)),
            const((1, LRU_WIDTH)),
            const((1, LRU_WIDTH)),
        ],
        out_specs=[
            feat_major,
            tok(ATTN_WIDTH),
            feat_major,
            pl.BlockSpec((1, ATTN_HEADS * nb, tm), lambda b, s: (b, 0, s)),
            tok(LRU_WIDTH),
        ],
        out_shape=[
            jax.ShapeDtypeStruct((B, ATTN_WIDTH, S), BF16),
            jax.ShapeDtypeStruct((B, S, ATTN_WIDTH), BF16),
            jax.ShapeDtypeStruct((B, ATTN_WIDTH, S), BF16),
            jax.ShapeDtypeStruct((B, ATTN_HEADS * nb, S), BF16),
            jax.ShapeDtypeStruct((B, S, LRU_WIDTH), BF16),
        ],
        scratch_shapes=[
            pltpu.VMEM((tm + V7X_SUBLANES, LRU_WIDTH), F32),
            pltpu.VMEM((V7X_SUBLANES, LRU_WIDTH), F32),
            pltpu.VMEM((nb, ATTN_WIDTH), F32),
        ],
        compiler_params=pltpu.CompilerParams(
            dimension_semantics=("arbitrary", "arbitrary"), vmem_limit_bytes=V7X_VMEM_LIMIT_BYTES),
        name="mixer_in",
    )(x, mod, g1, w_in, gq, gk, gmat, cw, cb, wg, bg, lam, go)


def _select_bias(gate, n_past):
    nb = gate.shape[0]
    blk = lax.broadcasted_iota(jnp.int32, gate.shape, 0)
    neg_inf = jnp.float32(-jnp.inf)
    g = jnp.where(blk < n_past, gate, neg_inf)
    bias = jnp.full(gate.shape, MASK_NEG, F32)
    for _ in range(MOBA_TOPK):
        mx = jnp.max(g, axis=0, keepdims=True)
        cand = jnp.where((g == mx) & (mx > neg_inf), blk, nb)
        pick = blk == jnp.min(cand, axis=0, keepdims=True)
        bias = jnp.where(pick, 0.0, bias)
        g = jnp.where(pick, neg_inf, g)
    return bias


def _moba_kernel(q_ref, k_ref, v_ref, bias_ref, blk_ref, o_ref, s_even, s_odd, mx_even, mx_odd):
    tq = q_ref.shape[2]
    nb = bias_ref.shape[1] // HEADS_PER_PAIR
    i = pl.program_id(2)
    feat = lax.broadcasted_iota(jnp.int32, (V7X_LANES, 1), 0)
    q_t = q_ref[0]

    def v_ext(h, start, n):
        vh = v_ref[0, h * HEAD_DIM:(h + 1) * HEAD_DIM, pl.ds(start, n)]
        return jnp.concatenate([vh, jnp.ones((2 * V7X_SUBLANES, n), BF16)], axis=0)

    start = pl.multiple_of(i * MOBA_BLOCK, MOBA_BLOCK)
    kd = k_ref[0, pl.ds(start, MOBA_BLOCK), :]
    key_pos = lax.broadcasted_iota(jnp.int32, (MOBA_BLOCK, tq), 0)
    qry_pos = lax.broadcasted_iota(jnp.int32, (MOBA_BLOCK, tq), 1)
    feat_pad = jnp.zeros((V7X_LANES - nb, tq), BF16)

    q_heads = [jnp.where((feat >= h * HEAD_DIM) & (feat < (h + 1) * HEAD_DIM), q_t, jnp.zeros_like(q_t))
               for h in range(HEADS_PER_PAIR)]
    qx = jnp.concatenate(
        [jnp.concatenate([q_heads[h], bias_ref[0, h * nb:(h + 1) * nb, :], feat_pad], axis=0)
         for h in range(HEADS_PER_PAIR)], axis=1)

    ck = KV_CHUNK_BLOCKS * MOBA_BLOCK
    n_chunks = (i + KV_CHUNK_BLOCKS - 1) // KV_CHUNK_BLOCKS

    def block_scores(c, t):
        ks = pl.multiple_of(c * ck + t * MOBA_BLOCK, MOBA_BLOCK)
        kx = jnp.concatenate([k_ref[0, pl.ds(ks, MOBA_BLOCK), :], blk_ref[pl.ds(ks, MOBA_BLOCK), :]], axis=1)
        return _dot(kx, qx)

    def run_step(c, cur, nxt, carry):
        s_ref, mx_ref = cur
        m_new = [jnp.maximum(carry[2 * h], mx_ref[0:1, h * tq:(h + 1) * tq]) for h in range(HEADS_PER_PAIR)]
        acc = [jnp.exp(carry[2 * h] - m_new[h]) * carry[2 * h + 1] for h in range(HEADS_PER_PAIR)]
        nxt_max = None
        for t in range(KV_CHUNK_BLOCKS):
            rows = slice(t * MOBA_BLOCK, (t + 1) * MOBA_BLOCK)
            if nxt is not None:
                s_nxt = block_scores(c + 1, t)
                nxt[0][rows, :] = s_nxt
                blk_max = jnp.max(s_nxt, axis=0, keepdims=True)
                nxt_max = blk_max if nxt_max is None else jnp.maximum(nxt_max, blk_max)
            ks = pl.multiple_of(c * ck + t * MOBA_BLOCK, MOBA_BLOCK)
            for h in range(HEADS_PER_PAIR):
                p = jnp.exp(s_ref[rows, h * tq:(h + 1) * tq] - m_new[h]).astype(BF16)
                acc[h] = acc[h] + _dot(v_ext(h, ks, MOBA_BLOCK), p)
        if nxt is not None:
            nxt[1][0:1, :] = nxt_max
        return (m_new[0], acc[0], m_new[1], acc[1])

    even, odd = (s_even, mx_even), (s_odd, mx_odd)

    s_own = [jnp.where(key_pos <= qry_pos, _dot(kd, q_heads[h]), MASK_NEG) for h in range(HEADS_PER_PAIR)]
    first_max = None
    for t in range(KV_CHUNK_BLOCKS):
        s_first = block_scores(0, t)
        s_even[t * MOBA_BLOCK:(t + 1) * MOBA_BLOCK, :] = s_first
        blk_max = jnp.max(s_first, axis=0, keepdims=True)
        first_max = blk_max if first_max is None else jnp.maximum(first_max, blk_max)
    mx_even[0:1, :] = first_max
    carry = []
    for h in range(HEADS_PER_PAIR):
        m = jnp.max(s_own[h], axis=0, keepdims=True)
        p = jnp.exp(s_own[h] - m).astype(BF16)
        carry += [m, _dot(v_ext(h, start, MOBA_BLOCK), p)]

    def step(c, carry, prefetch):
        return lax.cond(c % 2 == 0,
                        lambda: run_step(c, even, odd if prefetch else None, carry),
                        lambda: run_step(c, odd, even if prefetch else None, carry))

    carry = lax.fori_loop(0, n_chunks - 1, lambda c, carry: step(c, carry, True), tuple(carry))
    carry = lax.cond(n_chunks > 0, lambda: step(n_chunks - 1, carry, False), lambda: carry)

    heads = []
    for h in range(HEADS_PER_PAIR):
        acc = carry[2 * h + 1]
        heads.append(acc[0:HEAD_DIM] / acc[HEAD_DIM:HEAD_DIM + 1])
    o_ref[0] = jnp.concatenate(heads, axis=0).T


def _moba(q, k, v, bias):
    B, S, _ = k.shape
    nb = S // MOBA_BLOCK
    tq = MOBA_BLOCK
    ck = KV_CHUNK_BLOCKS * MOBA_BLOCK
    block_onehot = (jnp.arange(S)[:, None] // MOBA_BLOCK == jnp.arange(V7X_LANES)[None, :]).astype(BF16)
    return pl.pallas_call(
        _moba_kernel,
        grid=(B, N_PAIRS, S // tq),
        in_specs=[
            pl.BlockSpec((1, V7X_LANES, tq), lambda b, p, i: (b, p, i)),
            pl.BlockSpec((1, S, V7X_LANES), lambda b, p, i: (b, 0, p)),
            pl.BlockSpec((1, V7X_LANES, S), lambda b, p, i: (b, p, 0)),
            pl.BlockSpec((1, HEADS_PER_PAIR * nb, tq), lambda b, p, i: (b, p, i)),
            pl.BlockSpec((S, V7X_LANES), lambda b, p, i: (0, 0), pipeline_mode=pl.Buffered(1)),
        ],
        out_specs=pl.BlockSpec((1, tq, V7X_LANES), lambda b, p, i: (b, i, p)),
        out_shape=jax.ShapeDtypeStruct((B, S, ATTN_WIDTH), F32),
        scratch_shapes=[
            pltpu.VMEM((ck, HEADS_PER_PAIR * tq), F32),
            pltpu.VMEM((ck, HEADS_PER_PAIR * tq), F32),
            pltpu.VMEM((V7X_SUBLANES, HEADS_PER_PAIR * tq), F32),
            pltpu.VMEM((V7X_SUBLANES, HEADS_PER_PAIR * tq), F32),
        ],
        compiler_params=pltpu.CompilerParams(
            dimension_semantics=("arbitrary", "arbitrary", "arbitrary"),
            vmem_limit_bytes=V7X_VMEM_LIMIT_BYTES),
        name="moba",
    )(q, k, v, bias, block_onehot)


def _mixer_out_kernel(x_ref, lru_ref, attn_ref, mod_ref, ga_ref, w_out_ref, g2_ref, w_up_ref, cw_ref, cb_ref,
                      w_down_ref, o_ref, up_ext):
    tm = x_ref.shape[1]

    @pl.when(pl.program_id(1) == 0)
    def _():
        up_ext[0:V7X_SUBLANES, :] = jnp.zeros((V7X_SUBLANES, 2 * D_FF), F32)

    mod = mod_ref[0]
    gate1, sh2, sc2, gate2 = mod[2:3, :], mod[3:4, :], mod[4:5, :], mod[5:6, :]

    attn = attn_ref[0]
    attn_n = (attn * _rms_scale(attn) * ga_ref[...]).astype(BF16)
    mixed = _dot(lru_ref[0], w_out_ref[0:LRU_WIDTH, :]) + _dot(attn_n, w_out_ref[LRU_WIDTH:, :])
    x1 = x_ref[0] + gate1 * mixed

    h2 = ((x1 * _rms_scale(x1) * g2_ref[...]) * (1.0 + sc2) + sh2).astype(BF16)

    def conv_cols(c0):
        cols = pl.ds(c0, FF_CHUNK)
        up_ext[V7X_SUBLANES:, cols] = _dot(h2, w_up_ref[:, cols])
        out = cb_ref[:, cols] + cw_ref[FFN_CONV - 1:FFN_CONV, cols] * up_ext[V7X_SUBLANES:, cols]
        for j in range(FFN_CONV - 1):
            back = FFN_CONV - 1 - j
            out = out + cw_ref[j:j + 1, cols] * up_ext[pl.ds(V7X_SUBLANES - back, tm), cols]
        return out

    acc = jnp.zeros((tm, D_MODEL), F32)
    for c in range(D_FF // FF_CHUNK):
        g = conv_cols(c * FF_CHUNK)
        val = conv_cols(D_FF + c * FF_CHUNK)
        act = (g * jax.nn.sigmoid(g) * val).astype(BF16)
        acc = acc + _dot(act, w_down_ref[c * FF_CHUNK:(c + 1) * FF_CHUNK, :])
    up_ext[0:V7X_SUBLANES, :] = up_ext[tm:tm + V7X_SUBLANES, :]

    o_ref[0] = x1 + gate2 * acc


def _mixer_out(x, lru, attn, mod, ga, w_out, g2, w_up, cw, cb, w_down):
    B, S, _ = x.shape
    tm = TM_OUT
    const = lambda shape: pl.BlockSpec(shape, lambda b, s: (0,) * len(shape), pipeline_mode=pl.Buffered(1))
    tok = lambda w: pl.BlockSpec((1, tm, w), lambda b, s: (b, s, 0))
    return pl.pallas_call(
        _mixer_out_kernel,
        grid=(B, S // tm),
        in_specs=[
            tok(D_MODEL),
            tok(LRU_WIDTH),
            tok(ATTN_WIDTH),
            pl.BlockSpec((1, N_MOD, D_MODEL), lambda b, s: (b, 0, 0)),
            const((1, ATTN_WIDTH)),
            const((D_MODEL, D_MODEL)),
            const((1, D_MODEL)),
            const((D_MODEL, 2 * D_FF)),
            const((FFN_CONV, 2 * D_FF)),
            const((1, 2 * D_FF)),
            const((D_FF, D_MODEL)),
        ],
        out_specs=tok(D_MODEL),
        out_shape=jax.ShapeDtypeStruct((B, S, D_MODEL), F32),
        scratch_shapes=[pltpu.VMEM((tm + V7X_SUBLANES, 2 * D_FF), F32)],
        compiler_params=pltpu.CompilerParams(
            dimension_semantics=("arbitrary", "arbitrary"), vmem_limit_bytes=V7X_VMEM_LIMIT_BYTES),
        name="mixer_out",
    )(x, lru, attn, mod, ga, w_out, g2, w_up, cw, cb, w_down)


def _block_diag(w):
    n, d, _ = w.shape
    eye = jnp.eye(n, dtype=w.dtype)
    return jnp.einsum('hij,hg->higj', w, eye).reshape(n * d, n * d)


def _layer(x, c, w_ada, b_ada, norm1_g, w_in, q_norm_g, k_norm_g, lru_conv_w, lru_conv_b, lru_wa, lru_ba,
           lru_wx, lru_bx, lru_lambda, lru_out_g, attn_out_g, w_out, norm2_g, w_up, ffn_conv_w, ffn_conv_b,
           w_down):
    B, S, _ = x.shape
    row = lambda t: t.reshape(1, -1)
    mod = _modulation(c, w_ada, row(b_ada)).reshape(B, N_MOD, D_MODEL)

    head_id = jnp.arange(ATTN_WIDTH) // HEAD_DIM
    gmat = jnp.where(head_id[:, None] == head_id[None, :], 1.0 / HEAD_DIM, 0.0).astype(BF16)
    w_gates = jnp.concatenate([_block_diag(lru_wa), _block_diag(lru_wx)], axis=1).astype(BF16)
    b_gates = jnp.concatenate([lru_ba, lru_bx]).reshape(1, -1)

    q, k, v, bias, lru = _mixer_in(
        x, mod, row(norm1_g), w_in.astype(BF16), row(jnp.tile(q_norm_g, ATTN_HEADS)),
        row(jnp.tile(k_norm_g, ATTN_HEADS)), gmat, lru_conv_w, row(lru_conv_b), w_gates, b_gates,
        row(lru_lambda), row(lru_out_g))
    attn = _moba(q, k, v, bias)
    return _mixer_out(x, lru, attn, mod, row(attn_out_g), w_out.astype(BF16), row(norm2_g), w_up.astype(BF16),
                      ffn_conv_w, row(ffn_conv_b), w_down.astype(BF16))


def kernel(x, c, w_ada, b_ada, norm1_g, w_in, q_norm_g, k_norm_g, lru_conv_w, lru_conv_b, lru_wa, lru_ba, lru_wx, lru_bx, lru_lambda, lru_out_g, attn_out_g, w_out, norm2_g, w_up, ffn_conv_w, ffn_conv_b, w_down):
    depth = w_ada.shape[0]
    for l in range(depth):
        x = _layer(x, c, w_ada[l], b_ada[l], norm1_g[l], w_in[l], q_norm_g[l], k_norm_g[l], lru_conv_w[l],
                   lru_conv_b[l], lru_wa[l], lru_ba[l], lru_wx[l], lru_bx[l], lru_lambda[l], lru_out_g[l],
                   attn_out_g[l], w_out[l], norm2_g[l], w_up[l], ffn_conv_w[l], ffn_conv_b[l], w_down[l])
    return x
```

```python
import functools
import math

import jax
import jax.numpy as jnp
from jax import lax
from jax.experimental import pallas as pl
from jax.experimental.pallas import tpu as pltpu

D_MODEL = 1024
ATTN_HEADS = 8
HEAD_DIM = 64
ATTN_WIDTH = ATTN_HEADS * HEAD_DIM
MOBA_BLOCK = 256
MOBA_TOPK = 3
LRU_WIDTH = D_MODEL - ATTN_WIDTH
LRU_HEADS = 8
LRU_CONV = 4
LRU_C = 8.0
N_IN = 3 * ATTN_WIDTH + 2 * LRU_WIDTH
D_FF = 2816
FFN_CONV = 3
N_MOD = 6
EPS = 1e-6

V7X_LANES = 128
V7X_SUBLANES = 8
V7X_VMEM_LIMIT_BYTES = 56 * 1024 * 1024

HEADS_PER_PAIR = V7X_LANES // HEAD_DIM
N_PAIRS = ATTN_HEADS // HEADS_PER_PAIR
MASK_NEG = -1e30

TM_IN = 512
TM_OUT = 512
KV_CHUNK_BLOCKS = 4
FF_CHUNK = 1408
MOD_CHUNK = 1536

BF16 = jnp.bfloat16
F32 = jnp.float32


def _dot(a, b):
    return jnp.dot(a, b, preferred_element_type=F32)


def _rms_scale(y):
    return lax.rsqrt(jnp.mean(y * y, axis=-1, keepdims=True) + EPS)


def _mod_kernel(c_ref, w_ref, b_ref, o_ref):
    o_ref[...] = _dot(c_ref[...], w_ref[...]) + b_ref[...]


def _modulation(c, w_ada, b_ada):
    B = c.shape[0]
    n = w_ada.shape[1]
    return pl.pallas_call(
        _mod_kernel,
        grid=(n // MOD_CHUNK,),
        in_specs=[
            pl.BlockSpec((B, D_MODEL), lambda j: (0, 0)),
            pl.BlockSpec((D_MODEL, MOD_CHUNK), lambda j: (0, j)),
            pl.BlockSpec((1, MOD_CHUNK), lambda j: (0, j)),
        ],
        out_specs=pl.BlockSpec((B, MOD_CHUNK), lambda j: (0, j)),
        out_shape=jax.ShapeDtypeStruct((B, n), F32),
        compiler_params=pltpu.CompilerParams(
            dimension_semantics=("arbitrary",), vmem_limit_bytes=V7X_VMEM_LIMIT_BYTES),
        name="modulation",
    )(c, w_ada, b_ada)


def _head_rmsnorm(t, gmat, gain):
    sq = t * t
    hi = sq.astype(BF16)
    lo = (sq - hi.astype(F32)).astype(BF16)
    ms = _dot(hi, gmat) + _dot(lo, gmat)
    return t * lax.rsqrt(ms + EPS) * gain


def _shift_rows(t, d, fill):
    tm = t.shape[0]
    if d % V7X_SUBLANES == 0:
        pad = jnp.full((d, t.shape[1]), fill, t.dtype)
        return jnp.concatenate([pad, t[: tm - d]], axis=0)
    row = lax.broadcasted_iota(jnp.int32, t.shape, 0)
    return jnp.where(row >= d, pltpu.roll(t, d, 0), fill)


def _linear_scan(a, u):
    tm = a.shape[0]
    d = 1
    while d < tm:
        u = a * _shift_rows(u, d, 0.0) + u
        a = a * _shift_rows(a, d, 1.0)
        d *= 2
    return a, u


def _gelu_tanh(t):
    return 0.5 * t * (1.0 + jnp.tanh(math.sqrt(2.0 / math.pi) * (t + 0.044715 * (t * t * t))))


def _mixer_in_kernel(x_ref, mod_ref, g1_ref, w_in_ref, gq_ref, gk_ref, gmat_ref, cw_ref, cb_ref,
                     wg_ref, bg_ref, lam_ref, go_ref,
                     q_ref, k_ref, v_ref, bias_ref, lru_ref,
                     xr_ext, h_carry, km_all):
    @pl.when(pl.program_id(1) == 0)
    def _():
        xr_ext[0:V7X_SUBLANES, :] = jnp.zeros((V7X_SUBLANES, LRU_WIDTH), F32)
        h_carry[...] = jnp.zeros_like(h_carry)
        km_all[...] = jnp.zeros_like(km_all)

    blocks_per_step = x_ref.shape[1] // MOBA_BLOCK
    for sub in range(blocks_per_step):
        _mixer_in_block(sub, pl.program_id(1) * blocks_per_step + sub,
                        x_ref, mod_ref, g1_ref, w_in_ref, gq_ref, gk_ref, gmat_ref, cw_ref, cb_ref,
                        wg_ref, bg_ref, lam_ref, go_ref, q_ref, k_ref, v_ref, bias_ref, lru_ref,
                        xr_ext, h_carry, km_all)


def _mixer_in_block(sub, blk_id, x_ref, mod_ref, g1_ref, w_in_ref, gq_ref, gk_ref, gmat_ref, cw_ref, cb_ref,
                    wg_ref, bg_ref, lam_ref, go_ref, q_ref, k_ref, v_ref, bias_ref, lru_ref,
                    xr_ext, h_carry, km_all):
    tm = MOBA_BLOCK
    rows = slice(sub * tm, (sub + 1) * tm)
    x = x_ref[0, rows, :]
    mod = mod_ref[0]
    sh1 = mod[0:1, :]
    sc1 = mod[1:2, :]
    h = (x * _rms_scale(x)) * (g1_ref[...] * (1.0 + sc1)) + sh1
    hb = h.astype(BF16)

    def proj(j):
        return _dot(hb, w_in_ref[:, j * ATTN_WIDTH:(j + 1) * ATTN_WIDTH])

    gmat = gmat_ref[...]
    qn = _head_rmsnorm(proj(0), gmat, gq_ref[...] * (HEAD_DIM ** -0.5))
    q_t = qn.T.astype(BF16)
    q_ref[0, :, rows] = q_t

    nb = km_all.shape[0]
    km = km_all[...].astype(BF16)
    feat = lax.broadcasted_iota(jnp.int32, (V7X_LANES, 1), 0)
    for head in range(ATTN_HEADS):
        p, h = divmod(head, HEADS_PER_PAIR)
        q_pair = q_t[p * V7X_LANES:(p + 1) * V7X_LANES]
        qh = jnp.where((feat >= h * HEAD_DIM) & (feat < (h + 1) * HEAD_DIM), q_pair, jnp.zeros_like(q_pair))
        gate = _dot(km[:, p * V7X_LANES:(p + 1) * V7X_LANES], qh)
        bias_ref[0, head * nb:(head + 1) * nb, rows] = _select_bias(gate, blk_id).astype(BF16)

    kn = _head_rmsnorm(proj(1), gmat, gk_ref[...])
    k_ref[0, rows, :] = kn.astype(BF16)
    km_all[pl.ds(blk_id, 1), :] = jnp.mean(kn, axis=0, keepdims=True)
    v_ref[0, :, rows] = proj(2).T.astype(BF16)

    xr_ext[V7X_SUBLANES:, :] = proj(3)
    xc = cb_ref[...] + cw_ref[LRU_CONV - 1:LRU_CONV, :] * xr_ext[V7X_SUBLANES:, :]
    for j in range(LRU_CONV - 1):
        back = LRU_CONV - 1 - j
        xc = xc + cw_ref[j:j + 1, :] * xr_ext[pl.ds(V7X_SUBLANES - back, tm), :]
    xr_ext[0:V7X_SUBLANES, :] = xr_ext[tm:tm + V7X_SUBLANES, :]

    gates = _dot(xc.astype(BF16), wg_ref[...]) + bg_ref[...]
    r = jax.nn.sigmoid(gates[:, :LRU_WIDTH])
    i = jax.nn.sigmoid(gates[:, LRU_WIDTH:])
    neg_lam = -lam_ref[...]
    softplus = jnp.maximum(neg_lam, 0.0) + jnp.log1p(jnp.exp(-jnp.abs(neg_lam)))
    log_a = r * ((-LRU_C) * softplus)
    a = jnp.exp(log_a)
    th = jnp.tanh(log_a)
    u = jnp.sqrt(-2.0 * th / (1.0 - th)) * (i * xc)
    a_run, h_loc = _linear_scan(a, u)
    hs = h_loc + a_run * h_carry[0:1, :]
    h_carry[0:1, :] = hs[tm - 1:tm, :]

    y = hs * _gelu_tanh(proj(4))
    lru_ref[0, rows, :] = (y * _rms_scale(y) * go_ref[...]).astype(BF16)


def _mixer_in(x, mod, g1, w_in, gq, gk, gmat, cw, cb, wg, bg, lam, go):
    B, S, _ = x.shape
    tm = TM_IN
    assert tm % MOBA_BLOCK == 0, "a grid step covers whole MoBA blocks: the block choice reads the means of earlier blocks"
    nb = S // MOBA_BLOCK
    const = lambda shape: pl.BlockSpec(shape, lambda b, s: (0,) * len(shape))
    tok = lambda w: pl.BlockSpec((1, tm, w), lambda b, s: (b, s, 0))
    feat_major = pl.BlockSpec((1, ATTN_WIDTH, tm), lambda b, s: (b, 0, s))
    return pl.pallas_call(
        _mixer_in_kernel,
        grid=(B, S // tm),
        in_specs=[
            tok(D_MODEL),
            pl.BlockSpec((1, N_MOD, D_MODEL), lambda b, s: (b, 0, 0)),
            const((1, D_MODEL)),
            const((D_MODEL, N_IN)),
            const((1, ATTN_WIDTH)),
            const((1, ATTN_WIDTH)),
            const((ATTN_WIDTH, ATTN_WIDTH)),
            const((LRU_CONV, LRU_WIDTH)),
            const((1, LRU_WIDTH)),
            const((LRU_WIDTH, 2 * LRU_WIDTH)),
            const((1, 2 * LRU_WIDTH)),
            const((1, LRU_WIDTH)),
            const((1, LRU_WIDTH)),
        ],
        out_specs=[
            feat_major,
            tok(ATTN_WIDTH),
            feat_major,
            pl.BlockSpec((1, ATTN_HEADS * nb, tm), lambda b, s: (b, 0, s)),
            tok(LRU_WIDTH),
        ],
        out_shape=[
            jax.ShapeDtypeStruct((B, ATTN_WIDTH, S), BF16),
            jax.ShapeDtypeStruct((B, S, ATTN_WIDTH), BF16),
            jax.ShapeDtypeStruct((B, ATTN_WIDTH, S), BF16),
            jax.ShapeDtypeStruct((B, ATTN_HEADS * nb, S), BF16),
            jax.ShapeDtypeStruct((B, S, LRU_WIDTH), BF16),
        ],
        scratch_shapes=[
            pltpu.VMEM((MOBA_BLOCK + V7X_SUBLANES, LRU_WIDTH), F32),
            pltpu.VMEM((V7X_SUBLANES, LRU_WIDTH), F32),
            pltpu.VMEM((nb, ATTN_WIDTH), F32),
        ],
        compiler_params=pltpu.CompilerParams(
            dimension_semantics=("arbitrary", "arbitrary"), vmem_limit_bytes=V7X_VMEM_LIMIT_BYTES),
        name="mixer_in",
    )(x, mod, g1, w_in, gq, gk, gmat, cw, cb, wg, bg, lam, go)


def _select_bias(gate, n_past):
    nb = gate.shape[0]
    blk = lax.broadcasted_iota(jnp.int32, gate.shape, 0)
    neg_inf = jnp.float32(-jnp.inf)
    g = jnp.where(blk < n_past, gate, neg_inf)
    bias = jnp.full(gate.shape, MASK_NEG, F32)
    for _ in range(MOBA_TOPK):
        mx = jnp.max(g, axis=0, keepdims=True)
        cand = jnp.where((g == mx) & (mx > neg_inf), blk, nb)
        pick = blk == jnp.min(cand, axis=0, keepdims=True)
        bias = jnp.where(pick, 0.0, bias)
        g = jnp.where(pick, neg_inf, g)
    return bias


def _moba_kernel(q_ref, k_ref, v_ref, bias_ref, blk_ref, o_ref, s_even, s_odd, mx_even, mx_odd):
    tq = q_ref.shape[2]
    nb = bias_ref.shape[1] // HEADS_PER_PAIR
    i = pl.program_id(2)
    feat = lax.broadcasted_iota(jnp.int32, (V7X_LANES, 1), 0)
    q_t = q_ref[0]

    def v_ext(h, start, n):
        vh = v_ref[0, h * HEAD_DIM:(h + 1) * HEAD_DIM, pl.ds(start, n)]
        return jnp.concatenate([vh, jnp.ones((2 * V7X_SUBLANES, n), BF16)], axis=0)

    start = pl.multiple_of(i * MOBA_BLOCK, MOBA_BLOCK)
    kd = k_ref[0, pl.ds(start, MOBA_BLOCK), :]
    key_pos = lax.broadcasted_iota(jnp.int32, (MOBA_BLOCK, tq), 0)
    qry_pos = lax.broadcasted_iota(jnp.int32, (MOBA_BLOCK, tq), 1)
    feat_pad = jnp.zeros((V7X_LANES - nb, tq), BF16)

    q_heads = [jnp.where((feat >= h * HEAD_DIM) & (feat < (h + 1) * HEAD_DIM), q_t, jnp.zeros_like(q_t))
               for h in range(HEADS_PER_PAIR)]
    qx = jnp.concatenate(
        [jnp.concatenate([q_heads[h], bias_ref[0, h * nb:(h + 1) * nb, :], feat_pad], axis=0)
         for h in range(HEADS_PER_PAIR)], axis=1)

    ck = KV_CHUNK_BLOCKS * MOBA_BLOCK
    n_chunks = (i + KV_CHUNK_BLOCKS - 1) // KV_CHUNK_BLOCKS

    def block_scores(c, t):
        ks = pl.multiple_of(c * ck + t * MOBA_BLOCK, MOBA_BLOCK)
        kx = jnp.concatenate([k_ref[0, pl.ds(ks, MOBA_BLOCK), :], blk_ref[pl.ds(ks, MOBA_BLOCK), :]], axis=1)
        return _dot(kx, qx)

    def run_step(c, cur, nxt, carry):
        s_ref, mx_ref = cur
        m = [carry[2 * h] for h in range(HEADS_PER_PAIR)]
        acc = [carry[2 * h + 1] for h in range(HEADS_PER_PAIR)]
        for t in range(KV_CHUNK_BLOCKS):
            rows = slice(t * MOBA_BLOCK, (t + 1) * MOBA_BLOCK)
            if nxt is not None:
                s_nxt = block_scores(c + 1, t)
                nxt[0][rows, :] = s_nxt
                nxt[1][t:t + 1, :] = jnp.max(s_nxt, axis=0, keepdims=True)
            ks = pl.multiple_of(c * ck + t * MOBA_BLOCK, MOBA_BLOCK)
            for h in range(HEADS_PER_PAIR):
                cols = slice(h * tq, (h + 1) * tq)
                m_new = jnp.maximum(m[h], mx_ref[t:t + 1, cols])
                p = jnp.exp(s_ref[rows, cols] - m_new).astype(BF16)
                acc[h] = jnp.exp(m[h] - m_new) * acc[h] + _dot(v_ext(h, ks, MOBA_BLOCK), p)
                m[h] = m_new
        return (m[0], acc[0], m[1], acc[1])

    even, odd = (s_even, mx_even), (s_odd, mx_odd)

    s_own = [jnp.where(key_pos <= qry_pos, _dot(kd, q_heads[h]), MASK_NEG) for h in range(HEADS_PER_PAIR)]
    for t in range(KV_CHUNK_BLOCKS):
        s_first = block_scores(0, t)
        s_even[t * MOBA_BLOCK:(t + 1) * MOBA_BLOCK, :] = s_first
        mx_even[t:t + 1, :] = jnp.max(s_first, axis=0, keepdims=True)
    carry = []
    for h in range(HEADS_PER_PAIR):
        m = jnp.max(s_own[h], axis=0, keepdims=True)
        p = jnp.exp(s_own[h] - m).astype(BF16)
        carry += [m, _dot(v_ext(h, start, MOBA_BLOCK), p)]

    def pair(k, carry):
        carry = run_step(2 * k, even, odd, carry)
        return run_step(2 * k + 1, odd, even, carry)

    carry = lax.fori_loop(0, jnp.maximum(n_chunks - 1, 0) // 2, pair, tuple(carry))

    def last_two(carry):
        carry = run_step(n_chunks - 2, even, odd, carry)
        return run_step(n_chunks - 1, odd, None, carry)

    carry = lax.cond(
        n_chunks % 2 == 1,
        lambda: run_step(n_chunks - 1, even, None, carry),
        lambda: lax.cond(n_chunks > 0, lambda: last_two(carry), lambda: carry))

    heads = []
    for h in range(HEADS_PER_PAIR):
        acc = carry[2 * h + 1]
        heads.append(acc[0:HEAD_DIM] / acc[HEAD_DIM:HEAD_DIM + 1])
    o_ref[0] = jnp.concatenate(heads, axis=0).T


def _moba(q, k, v, bias):
    B, S, _ = k.shape
    nb = S // MOBA_BLOCK
    tq = MOBA_BLOCK
    ck = KV_CHUNK_BLOCKS * MOBA_BLOCK
    block_onehot = (jnp.arange(S)[:, None] // MOBA_BLOCK == jnp.arange(V7X_LANES)[None, :]).astype(BF16)
    return pl.pallas_call(
        _moba_kernel,
        grid=(B, N_PAIRS, S // tq),
        in_specs=[
            pl.BlockSpec((1, V7X_LANES, tq), lambda b, p, i: (b, p, i)),
            pl.BlockSpec((1, S, V7X_LANES), lambda b, p, i: (b, 0, p)),
            pl.BlockSpec((1, V7X_LANES, S), lambda b, p, i: (b, p, 0)),
            pl.BlockSpec((1, HEADS_PER_PAIR * nb, tq), lambda b, p, i: (b, p, i)),
            pl.BlockSpec((S, V7X_LANES), lambda b, p, i: (0, 0), pipeline_mode=pl.Buffered(1)),
        ],
        out_specs=pl.BlockSpec((1, tq, V7X_LANES), lambda b, p, i: (b, i, p)),
        out_shape=jax.ShapeDtypeStruct((B, S, ATTN_WIDTH), F32),
        scratch_shapes=[
            pltpu.VMEM((ck, HEADS_PER_PAIR * tq), F32),
            pltpu.VMEM((ck, HEADS_PER_PAIR * tq), F32),
            pltpu.VMEM((V7X_SUBLANES, HEADS_PER_PAIR * tq), F32),
            pltpu.VMEM((V7X_SUBLANES, HEADS_PER_PAIR * tq), F32),
        ],
        compiler_params=pltpu.CompilerParams(
            dimension_semantics=("arbitrary", "arbitrary", "arbitrary"),
            vmem_limit_bytes=V7X_VMEM_LIMIT_BYTES),
        name="moba",
    )(q, k, v, bias, block_onehot)


def _mixer_out_kernel(x_ref, lru_ref, attn_ref, mod_ref, ga_ref, w_out_ref, g2_ref, w_up_ref, cw_ref, cb_ref,
                      w_down_ref, o_ref, up_ext):
    tm = x_ref.shape[1]

    @pl.when(pl.program_id(1) == 0)
    def _():
        up_ext[0:V7X_SUBLANES, :] = jnp.zeros((V7X_SUBLANES, 2 * D_FF), F32)

    mod = mod_ref[0]
    gate1, sh2, sc2, gate2 = mod[2:3, :], mod[3:4, :], mod[4:5, :], mod[5:6, :]

    attn = attn_ref[0]
    attn_n = (attn * _rms_scale(attn) * ga_ref[...]).astype(BF16)
    mixed = _dot(lru_ref[0], w_out_ref[0:LRU_WIDTH, :]) + _dot(attn_n, w_out_ref[LRU_WIDTH:, :])
    x1 = x_ref[0] + gate1 * mixed

    h2 = ((x1 * _rms_scale(x1) * g2_ref[...]) * (1.0 + sc2) + sh2).astype(BF16)

    def conv_cols(c0):
        cols = pl.ds(c0, FF_CHUNK)
        up_ext[V7X_SUBLANES:, cols] = _dot(h2, w_up_ref[:, cols])
        out = cb_ref[:, cols] + cw_ref[FFN_CONV - 1:FFN_CONV, cols] * up_ext[V7X_SUBLANES:, cols]
        for j in range(FFN_CONV - 1):
            back = FFN_CONV - 1 - j
            out = out + cw_ref[j:j + 1, cols] * up_ext[pl.ds(V7X_SUBLANES - back, tm), cols]
        return out

    acc = jnp.zeros((tm, D_MODEL), F32)
    for c in range(D_FF // FF_CHUNK):
        g = conv_cols(c * FF_CHUNK)
        val = conv_cols(D_FF + c * FF_CHUNK)
        act = (g * jax.nn.sigmoid(g) * val).astype(BF16)
        acc = acc + _dot(act, w_down_ref[c * FF_CHUNK:(c + 1) * FF_CHUNK, :])
    up_ext[0:V7X_SUBLANES, :] = up_ext[tm:tm + V7X_SUBLANES, :]

    o_ref[0] = x1 + gate2 * acc


def _mixer_out(x, lru, attn, mod, ga, w_out, g2, w_up, cw, cb, w_down):
    B, S, _ = x.shape
    tm = TM_OUT
    const = lambda shape: pl.BlockSpec(shape, lambda b, s: (0,) * len(shape), pipeline_mode=pl.Buffered(1))
    tok = lambda w: pl.BlockSpec((1, tm, w), lambda b, s: (b, s, 0))
    return pl.pallas_call(
        _mixer_out_kernel,
        grid=(B, S // tm),
        in_specs=[
            tok(D_MODEL),
            tok(LRU_WIDTH),
            tok(ATTN_WIDTH),
            pl.BlockSpec((1, N_MOD, D_MODEL), lambda b, s: (b, 0, 0)),
            const((1, ATTN_WIDTH)),
            const((D_MODEL, D_MODEL)),
            const((1, D_MODEL)),
            const((D_MODEL, 2 * D_FF)),
            const((FFN_CONV, 2 * D_FF)),
            const((1, 2 * D_FF)),
            const((D_FF, D_MODEL)),
        ],
        out_specs=tok(D_MODEL),
        out_shape=jax.ShapeDtypeStruct((B, S, D_MODEL), F32),
        scratch_shapes=[pltpu.VMEM((tm + V7X_SUBLANES, 2 * D_FF), F32)],
        compiler_params=pltpu.CompilerParams(
            dimension_semantics=("arbitrary", "arbitrary"), vmem_limit_bytes=V7X_VMEM_LIMIT_BYTES),
        name="mixer_out",
    )(x, lru, attn, mod, ga, w_out, g2, w_up, cw, cb, w_down)


def _block_diag(w):
    n, d, _ = w.shape
    eye = jnp.eye(n, dtype=w.dtype)
    return jnp.einsum('hij,hg->higj', w, eye).reshape(n * d, n * d)


def _layer(x, c, w_ada, b_ada, norm1_g, w_in, q_norm_g, k_norm_g, lru_conv_w, lru_conv_b, lru_wa, lru_ba,
           lru_wx, lru_bx, lru_lambda, lru_out_g, attn_out_g, w_out, norm2_g, w_up, ffn_conv_w, ffn_conv_b,
           w_down):
    B, S, _ = x.shape
    row = lambda t: t.reshape(1, -1)
    mod = _modulation(c, w_ada, row(b_ada)).reshape(B, N_MOD, D_MODEL)

    head_id = jnp.arange(ATTN_WIDTH) // HEAD_DIM
    gmat = jnp.where(head_id[:, None] == head_id[None, :], 1.0 / HEAD_DIM, 0.0).astype(BF16)
    w_gates = jnp.concatenate([_block_diag(lru_wa), _block_diag(lru_wx)], axis=1).astype(BF16)
    b_gates = jnp.concatenate([lru_ba, lru_bx]).reshape(1, -1)

    q, k, v, bias, lru = _mixer_in(
        x, mod, row(norm1_g), w_in.astype(BF16), row(jnp.tile(q_norm_g, ATTN_HEADS)),
        row(jnp.tile(k_norm_g, ATTN_HEADS)), gmat, lru_conv_w, row(lru_conv_b), w_gates, b_gates,
        row(lru_lambda), row(lru_out_g))
    attn = _moba(q, k, v, bias)
    return _mixer_out(x, lru, attn, mod, row(attn_out_g), w_out.astype(BF16), row(norm2_g), w_up.astype(BF16),
                      ffn_conv_w, row(ffn_conv_b), w_down.astype(BF16))


def kernel(x, c, w_ada, b_ada, norm1_g, w_in, q_norm_g, k_norm_g, lru_conv_w, lru_conv_b, lru_wa, lru_ba, lru_wx, lru_bx, lru_lambda, lru_out_g, attn_out_g, w_out, norm2_g, w_up, ffn_conv_w, ffn_conv_b, w_down):
    depth = w_ada.shape[0]
    for l in range(depth):
        x = _layer(x, c, w_ada[l], b_ada[l], norm1_g[l], w_in[l], q_norm_g[l], k_norm_g[l], lru_conv_w[l],
                   lru_conv_b[l], lru_wa[l], lru_ba[l], lru_wx[l], lru_bx[l], lru_lambda[l], lru_out_g[l],
                   attn_out_g[l], w_out[l], norm2_g[l], w_up[l], ffn_conv_w[l], ffn_conv_b[l], w_down[l])
    return x
```

```python
import functools
import math

import jax
import jax.numpy as jnp
from jax import lax
from jax.experimental import pallas as pl
from jax.experimental.pallas import tpu as pltpu

D_MODEL = 1024
ATTN_HEADS = 8
HEAD_DIM = 64
ATTN_WIDTH = ATTN_HEADS * HEAD_DIM
MOBA_BLOCK = 256
MOBA_TOPK = 3
LRU_WIDTH = D_MODEL - ATTN_WIDTH
LRU_HEADS = 8
LRU_CONV = 4
LRU_C = 8.0
N_IN = 3 * ATTN_WIDTH + 2 * LRU_WIDTH
D_FF = 2816
FFN_CONV = 3
N_MOD = 6
EPS = 1e-6

V7X_LANES = 128
V7X_SUBLANES = 8
V7X_VMEM_LIMIT_BYTES = 56 * 1024 * 1024

HEADS_PER_PAIR = V7X_LANES // HEAD_DIM
N_PAIRS = ATTN_HEADS // HEADS_PER_PAIR
MASK_NEG = -1e30

TM_IN = 512
TM_OUT = 512
KV_CHUNK_BLOCKS = 4
FF_CHUNK = 1408
MOD_CHUNK = 1536

BF16 = jnp.bfloat16
F32 = jnp.float32


def _dot(a, b):
    return jnp.dot(a, b, preferred_element_type=F32)


def _rms_scale(y):
    return lax.rsqrt(jnp.mean(y * y, axis=-1, keepdims=True) + EPS)


def _mod_kernel(c_ref, w_ref, b_ref, o_ref):
    o_ref[...] = _dot(c_ref[...], w_ref[...]) + b_ref[...]


def _modulation(c, w_ada, b_ada):
    B = c.shape[0]
    n = w_ada.shape[1]
    return pl.pallas_call(
        _mod_kernel,
        grid=(n // MOD_CHUNK,),
        in_specs=[
            pl.BlockSpec((B, D_MODEL), lambda j: (0, 0)),
            pl.BlockSpec((D_MODEL, MOD_CHUNK), lambda j: (0, j)),
            pl.BlockSpec((1, MOD_CHUNK), lambda j: (0, j)),
        ],
        out_specs=pl.BlockSpec((B, MOD_CHUNK), lambda j: (0, j)),
        out_shape=jax.ShapeDtypeStruct((B, n), F32),
        compiler_params=pltpu.CompilerParams(
            dimension_semantics=("arbitrary",), vmem_limit_bytes=V7X_VMEM_LIMIT_BYTES),
        name="modulation",
    )(c, w_ada, b_ada)


def _head_rmsnorm(t, gmat, gain):
    sq = t * t
    hi = sq.astype(BF16)
    lo = (sq - hi.astype(F32)).astype(BF16)
    ms = _dot(hi, gmat) + _dot(lo, gmat)
    return t * lax.rsqrt(ms + EPS) * gain


def _shift_rows(t, d, fill):
    tm = t.shape[0]
    if d % V7X_SUBLANES == 0:
        pad = jnp.full((d, t.shape[1]), fill, t.dtype)
        return jnp.concatenate([pad, t[: tm - d]], axis=0)
    row = lax.broadcasted_iota(jnp.int32, t.shape, 0)
    return jnp.where(row >= d, pltpu.roll(t, d, 0), fill)


SCAN_RUN = V7X_SUBLANES
SCAN_GROUP = SCAN_RUN * V7X_SUBLANES


def _to_run_major(buf, t):
    tm, width = t.shape
    lane_groups = width // V7X_LANES
    for l in range(lane_groups):
        buf[l] = t[:, l * V7X_LANES:(l + 1) * V7X_LANES]
    pieces = []
    for g in range(tm // SCAN_GROUP):
        for s in range(SCAN_RUN):
            rows = pl.ds(g * SCAN_GROUP + s, V7X_SUBLANES, stride=SCAN_RUN)
            pieces.append(jnp.concatenate([buf[l, rows, :] for l in range(lane_groups)], axis=1))
    return jnp.concatenate(pieces, axis=0)


def _from_run_major(buf, t):
    tm, width = t.shape
    lane_groups = width // V7X_LANES
    for g in range(tm // SCAN_GROUP):
        for s in range(SCAN_RUN):
            piece = t[g * SCAN_GROUP + s * V7X_SUBLANES:g * SCAN_GROUP + (s + 1) * V7X_SUBLANES]
            rows = pl.ds(g * SCAN_GROUP + s, V7X_SUBLANES, stride=SCAN_RUN)
            for l in range(lane_groups):
                buf[l, rows, :] = piece[:, l * V7X_LANES:(l + 1) * V7X_LANES]
    return jnp.concatenate([buf[l] for l in range(lane_groups)], axis=1)


def _linear_scan_run_major(a, u, h_in):
    tm = a.shape[0]
    row = lax.broadcasted_iota(jnp.int32, (V7X_SUBLANES, a.shape[1]), 0)
    out = []
    for g in range(tm // SCAN_GROUP):
        h_loc, a_run = [], []
        for s in range(SCAN_RUN):
            rows = slice(g * SCAN_GROUP + s * V7X_SUBLANES, g * SCAN_GROUP + (s + 1) * V7X_SUBLANES)
            h_loc.append(u[rows] if s == 0 else a[rows] * h_loc[-1] + u[rows])
            a_run.append(a[rows] if s == 0 else a[rows] * a_run[-1])
        h_end, a_end = h_loc[-1], a_run[-1]
        d = 1
        while d < V7X_SUBLANES:
            h_end = a_end * _shift_rows(h_end, d, 0.0) + h_end
            a_end = a_end * _shift_rows(a_end, d, 1.0)
            d *= 2
        after = h_end + a_end * h_in
        before = jnp.where(row >= 1, pltpu.roll(after, 1, 0), h_in)
        h_in = after[V7X_SUBLANES - 1:V7X_SUBLANES]
        out += [h_loc[s] + a_run[s] * before for s in range(SCAN_RUN)]
    return jnp.concatenate(out, axis=0), h_in


def _gelu_tanh(t):
    return 0.5 * t * (1.0 + jnp.tanh(math.sqrt(2.0 / math.pi) * (t + 0.044715 * (t * t * t))))


def _mixer_in_kernel(x_ref, mod_ref, g1_ref, w_in_ref, gq_ref, gk_ref, gmat_ref, cw_ref, cb_ref,
                     wg_ref, bg_ref, lam_ref, go_ref,
                     q_ref, k_ref, v_ref, bias_ref, lru_ref,
                     xr_ext, h_carry, km_all, scan_buf):
    @pl.when(pl.program_id(1) == 0)
    def _():
        xr_ext[0:V7X_SUBLANES, :] = jnp.zeros((V7X_SUBLANES, LRU_WIDTH), F32)
        h_carry[...] = jnp.zeros_like(h_carry)
        km_all[...] = jnp.zeros_like(km_all)

    blocks_per_step = x_ref.shape[1] // MOBA_BLOCK
    for sub in range(blocks_per_step):
        _mixer_in_block(sub, pl.program_id(1) * blocks_per_step + sub,
                        x_ref, mod_ref, g1_ref, w_in_ref, gq_ref, gk_ref, gmat_ref, cw_ref, cb_ref,
                        wg_ref, bg_ref, lam_ref, go_ref, q_ref, k_ref, v_ref, bias_ref, lru_ref,
                        xr_ext, h_carry, km_all, scan_buf)


def _mixer_in_block(sub, blk_id, x_ref, mod_ref, g1_ref, w_in_ref, gq_ref, gk_ref, gmat_ref, cw_ref, cb_ref,
                    wg_ref, bg_ref, lam_ref, go_ref, q_ref, k_ref, v_ref, bias_ref, lru_ref,
                    xr_ext, h_carry, km_all, scan_buf):
    tm = MOBA_BLOCK
    rows = slice(sub * tm, (sub + 1) * tm)
    x = x_ref[0, rows, :]
    mod = mod_ref[0]
    sh1 = mod[0:1, :]
    sc1 = mod[1:2, :]
    h = (x * _rms_scale(x)) * (g1_ref[...] * (1.0 + sc1)) + sh1
    hb = h.astype(BF16)

    def proj(j):
        return _dot(hb, w_in_ref[:, j * ATTN_WIDTH:(j + 1) * ATTN_WIDTH])

    gmat = gmat_ref[...]
    qn = _head_rmsnorm(proj(0), gmat, gq_ref[...] * (HEAD_DIM ** -0.5))
    q_t = qn.T.astype(BF16)
    q_ref[0, :, rows] = q_t

    nb = km_all.shape[0]
    km = km_all[...].astype(BF16)
    feat = lax.broadcasted_iota(jnp.int32, (V7X_LANES, 1), 0)
    for head in range(ATTN_HEADS):
        p, h = divmod(head, HEADS_PER_PAIR)
        q_pair = q_t[p * V7X_LANES:(p + 1) * V7X_LANES]
        qh = jnp.where((feat >= h * HEAD_DIM) & (feat < (h + 1) * HEAD_DIM), q_pair, jnp.zeros_like(q_pair))
        gate = _dot(km[:, p * V7X_LANES:(p + 1) * V7X_LANES], qh)
        bias_ref[0, head * nb:(head + 1) * nb, rows] = _select_bias(gate, blk_id).astype(BF16)

    kn = _head_rmsnorm(proj(1), gmat, gk_ref[...])
    k_ref[0, rows, :] = kn.astype(BF16)
    km_all[pl.ds(blk_id, 1), :] = jnp.mean(kn, axis=0, keepdims=True)
    v_ref[0, :, rows] = proj(2).T.astype(BF16)

    xr_ext[V7X_SUBLANES:, :] = proj(3)
    xc = cb_ref[...] + cw_ref[LRU_CONV - 1:LRU_CONV, :] * xr_ext[V7X_SUBLANES:, :]
    for j in range(LRU_CONV - 1):
        back = LRU_CONV - 1 - j
        xc = xc + cw_ref[j:j + 1, :] * xr_ext[pl.ds(V7X_SUBLANES - back, tm), :]
    xr_ext[0:V7X_SUBLANES, :] = xr_ext[tm:tm + V7X_SUBLANES, :]

    xc = _to_run_major(scan_buf, xc)
    gates = _dot(xc.astype(BF16), wg_ref[...]) + bg_ref[...]
    r = jax.nn.sigmoid(gates[:, :LRU_WIDTH])
    i = jax.nn.sigmoid(gates[:, LRU_WIDTH:])
    neg_lam = -lam_ref[...]
    softplus = jnp.maximum(neg_lam, 0.0) + jnp.log1p(jnp.exp(-jnp.abs(neg_lam)))
    log_a = r * ((-LRU_C) * softplus)
    a = jnp.exp(log_a)
    th = jnp.tanh(log_a)
    u = jnp.sqrt(-2.0 * th / (1.0 - th)) * (i * xc)
    hs, h_last = _linear_scan_run_major(a, u, h_carry[0:1, :])
    h_carry[0:1, :] = h_last
    hs = _from_run_major(scan_buf, hs)

    y = hs * _gelu_tanh(proj(4))
    lru_ref[0, rows, :] = (y * _rms_scale(y) * go_ref[...]).astype(BF16)


def _mixer_in(x, mod, g1, w_in, gq, gk, gmat, cw, cb, wg, bg, lam, go):
    B, S, _ = x.shape
    tm = TM_IN
    assert tm % MOBA_BLOCK == 0, "a grid step covers whole MoBA blocks: the block choice reads the means of earlier blocks"
    nb = S // MOBA_BLOCK
    const = lambda shape: pl.BlockSpec(shape, lambda b, s: (0,) * len(shape))
    tok = lambda w: pl.BlockSpec((1, tm, w), lambda b, s: (b, s, 0))
    feat_major = pl.BlockSpec((1, ATTN_WIDTH, tm), lambda b, s: (b, 0, s))
    return pl.pallas_call(
        _mixer_in_kernel,
        grid=(B, S // tm),
        in_specs=[
            tok(D_MODEL),
            pl.BlockSpec((1, N_MOD, D_MODEL), lambda b, s: (b, 0, 0)),
            const((1, D_MODEL)),
            const((D_MODEL, N_IN)),
            const((1, ATTN_WIDTH)),
            const((1, ATTN_WIDTH)),
            const((ATTN_WIDTH, ATTN_WIDTH)),
            const((LRU_CONV, LRU_WIDTH)),
            const((1, LRU_WIDTH)),
            const((LRU_WIDTH, 2 * LRU_WIDTH)),
            const((1, 2 * LRU_WIDTH)),
            const((1, LRU_WIDTH)),
            const((1, LRU_WIDTH)),
        ],
        out_specs=[
            feat_major,
            tok(ATTN_WIDTH),
            feat_major,
            pl.BlockSpec((1, ATTN_HEADS * nb, tm), lambda b, s: (b, 0, s)),
            tok(LRU_WIDTH),
        ],
        out_shape=[
            jax.ShapeDtypeStruct((B, ATTN_WIDTH, S), BF16),
            jax.ShapeDtypeStruct((B, S, ATTN_WIDTH), BF16),
            jax.ShapeDtypeStruct((B, ATTN_WIDTH, S), BF16),
            jax.ShapeDtypeStruct((B, ATTN_HEADS * nb, S), BF16),
            jax.ShapeDtypeStruct((B, S, LRU_WIDTH), BF16),
        ],
        scratch_shapes=[
            pltpu.VMEM((MOBA_BLOCK + V7X_SUBLANES, LRU_WIDTH), F32),
            pltpu.VMEM((V7X_SUBLANES, LRU_WIDTH), F32),
            pltpu.VMEM((nb, ATTN_WIDTH), F32),
            pltpu.VMEM((LRU_WIDTH // V7X_LANES, MOBA_BLOCK, V7X_LANES), F32),
        ],
        compiler_params=pltpu.CompilerParams(
            dimension_semantics=("arbitrary", "arbitrary"), vmem_limit_bytes=V7X_VMEM_LIMIT_BYTES),
        name="mixer_in",
    )(x, mod, g1, w_in, gq, gk, gmat, cw, cb, wg, bg, lam, go)


def _select_bias(gate, n_past):
    nb = gate.shape[0]
    blk = lax.broadcasted_iota(jnp.int32, gate.shape, 0)
    neg_inf = jnp.float32(-jnp.inf)
    g = jnp.where(blk < n_past, gate, neg_inf)
    bias = jnp.full(gate.shape, MASK_NEG, F32)
    for _ in range(MOBA_TOPK):
        mx = jnp.max(g, axis=0, keepdims=True)
        cand = jnp.where((g == mx) & (mx > neg_inf), blk, nb)
        pick = blk == jnp.min(cand, axis=0, keepdims=True)
        bias = jnp.where(pick, 0.0, bias)
        g = jnp.where(pick, neg_inf, g)
    return bias


def _moba_kernel(q_ref, k_ref, v_ref, bias_ref, blk_ref, o_ref, s_even, s_odd, mx_even, mx_odd):
    def tile(i, _):
        _moba_tile(i, q_ref, k_ref, v_ref, bias_ref, blk_ref, o_ref, s_even, s_odd, mx_even, mx_odd)
        return 0

    lax.fori_loop(0, q_ref.shape[2] // MOBA_BLOCK, tile, 0)


def _moba_tile(i, q_ref, k_ref, v_ref, bias_ref, blk_ref, o_ref, s_even, s_odd, mx_even, mx_odd):
    tq = MOBA_BLOCK
    nb = bias_ref.shape[1] // HEADS_PER_PAIR
    feat = lax.broadcasted_iota(jnp.int32, (V7X_LANES, 1), 0)
    start = pl.multiple_of(i * MOBA_BLOCK, MOBA_BLOCK)
    q_t = q_ref[0, :, pl.ds(start, tq)]

    def v_ext(h, start, n):
        vh = v_ref[0, h * HEAD_DIM:(h + 1) * HEAD_DIM, pl.ds(start, n)]
        return jnp.concatenate([vh, jnp.ones((2 * V7X_SUBLANES, n), BF16)], axis=0)

    kd = k_ref[0, pl.ds(start, MOBA_BLOCK), :]
    key_pos = lax.broadcasted_iota(jnp.int32, (MOBA_BLOCK, tq), 0)
    qry_pos = lax.broadcasted_iota(jnp.int32, (MOBA_BLOCK, tq), 1)
    feat_pad = jnp.zeros((V7X_LANES - nb, tq), BF16)

    q_heads = [jnp.where((feat >= h * HEAD_DIM) & (feat < (h + 1) * HEAD_DIM), q_t, jnp.zeros_like(q_t))
               for h in range(HEADS_PER_PAIR)]
    qx = jnp.concatenate(
        [jnp.concatenate([q_heads[h], bias_ref[0, h * nb:(h + 1) * nb, pl.ds(start, tq)], feat_pad], axis=0)
         for h in range(HEADS_PER_PAIR)], axis=1)

    ck = KV_CHUNK_BLOCKS * MOBA_BLOCK
    n_chunks = (i + KV_CHUNK_BLOCKS - 1) // KV_CHUNK_BLOCKS

    def block_scores(c, t):
        ks = pl.multiple_of(c * ck + t * MOBA_BLOCK, MOBA_BLOCK)
        kx = jnp.concatenate([k_ref[0, pl.ds(ks, MOBA_BLOCK), :], blk_ref[pl.ds(ks, MOBA_BLOCK), :]], axis=1)
        return _dot(kx, qx)

    def run_step(c, cur, nxt, carry):
        s_ref, mx_ref = cur
        m = [carry[2 * h] for h in range(HEADS_PER_PAIR)]
        acc = [carry[2 * h + 1] for h in range(HEADS_PER_PAIR)]
        for t in range(KV_CHUNK_BLOCKS):
            rows = slice(t * MOBA_BLOCK, (t + 1) * MOBA_BLOCK)
            if nxt is not None:
                s_nxt = block_scores(c + 1, t)
                nxt[0][rows, :] = s_nxt
                nxt[1][t:t + 1, :] = jnp.max(s_nxt, axis=0, keepdims=True)
            ks = pl.multiple_of(c * ck + t * MOBA_BLOCK, MOBA_BLOCK)
            for h in range(HEADS_PER_PAIR):
                cols = slice(h * tq, (h + 1) * tq)
                m_new = jnp.maximum(m[h], mx_ref[t:t + 1, cols])
                p = jnp.exp(s_ref[rows, cols] - m_new).astype(BF16)
                acc[h] = jnp.exp(m[h] - m_new) * acc[h] + _dot(v_ext(h, ks, MOBA_BLOCK), p)
                m[h] = m_new
        return (m[0], acc[0], m[1], acc[1])

    even, odd = (s_even, mx_even), (s_odd, mx_odd)

    s_own = [jnp.where(key_pos <= qry_pos, _dot(kd, q_heads[h]), MASK_NEG) for h in range(HEADS_PER_PAIR)]
    for t in range(KV_CHUNK_BLOCKS):
        s_first = block_scores(0, t)
        s_even[t * MOBA_BLOCK:(t + 1) * MOBA_BLOCK, :] = s_first
        mx_even[t:t + 1, :] = jnp.max(s_first, axis=0, keepdims=True)
    carry = []
    for h in range(HEADS_PER_PAIR):
        m = jnp.max(s_own[h], axis=0, keepdims=True)
        p = jnp.exp(s_own[h] - m).astype(BF16)
        carry += [m, _dot(v_ext(h, start, MOBA_BLOCK), p)]

    def pair(k, carry):
        carry = run_step(2 * k, even, odd, carry)
        return run_step(2 * k + 1, odd, even, carry)

    carry = lax.fori_loop(0, jnp.maximum(n_chunks - 1, 0) // 2, pair, tuple(carry))

    def last_two(carry):
        carry = run_step(n_chunks - 2, even, odd, carry)
        return run_step(n_chunks - 1, odd, None, carry)

    carry = lax.cond(
        n_chunks % 2 == 1,
        lambda: run_step(n_chunks - 1, even, None, carry),
        lambda: lax.cond(n_chunks > 0, lambda: last_two(carry), lambda: carry))

    heads = []
    for h in range(HEADS_PER_PAIR):
        acc = carry[2 * h + 1]
        heads.append(acc[0:HEAD_DIM] / acc[HEAD_DIM:HEAD_DIM + 1])
    o_ref[0, pl.ds(start, tq), :] = jnp.concatenate(heads, axis=0).T


def _moba(q, k, v, bias):
    B, S, _ = k.shape
    nb = S // MOBA_BLOCK
    tq = MOBA_BLOCK
    ck = KV_CHUNK_BLOCKS * MOBA_BLOCK
    block_onehot = (jnp.arange(S)[:, None] // MOBA_BLOCK == jnp.arange(V7X_LANES)[None, :]).astype(BF16)
    feat_major = pl.BlockSpec((1, V7X_LANES, S), lambda b, p: (b, p, 0))
    return pl.pallas_call(
        _moba_kernel,
        grid=(B, N_PAIRS),
        in_specs=[
            feat_major,
            pl.BlockSpec((1, S, V7X_LANES), lambda b, p: (b, 0, p)),
            feat_major,
            pl.BlockSpec((1, HEADS_PER_PAIR * nb, S), lambda b, p: (b, p, 0)),
            pl.BlockSpec((S, V7X_LANES), lambda b, p: (0, 0), pipeline_mode=pl.Buffered(1)),
        ],
        out_specs=pl.BlockSpec((1, S, V7X_LANES), lambda b, p: (b, 0, p)),
        out_shape=jax.ShapeDtypeStruct((B, S, ATTN_WIDTH), F32),
        scratch_shapes=[
            pltpu.VMEM((ck, HEADS_PER_PAIR * tq), F32),
            pltpu.VMEM((ck, HEADS_PER_PAIR * tq), F32),
            pltpu.VMEM((V7X_SUBLANES, HEADS_PER_PAIR * tq), F32),
            pltpu.VMEM((V7X_SUBLANES, HEADS_PER_PAIR * tq), F32),
        ],
        compiler_params=pltpu.CompilerParams(
            dimension_semantics=("arbitrary", "arbitrary"),
            vmem_limit_bytes=V7X_VMEM_LIMIT_BYTES),
        name="moba",
    )(q, k, v, bias, block_onehot)


def _mixer_out_kernel(x_ref, lru_ref, attn_ref, mod_ref, ga_ref, w_out_ref, g2_ref, w_up_ref, cw_ref, cb_ref,
                      w_down_ref, o_ref, up_ext):
    tm = x_ref.shape[1]

    @pl.when(pl.program_id(1) == 0)
    def _():
        up_ext[0:V7X_SUBLANES, :] = jnp.zeros((V7X_SUBLANES, 2 * D_FF), F32)

    mod = mod_ref[0]
    gate1, sh2, sc2, gate2 = mod[2:3, :], mod[3:4, :], mod[4:5, :], mod[5:6, :]

    attn = attn_ref[0]
    attn_n = (attn * _rms_scale(attn) * ga_ref[...]).astype(BF16)
    mixed = _dot(lru_ref[0], w_out_ref[0:LRU_WIDTH, :]) + _dot(attn_n, w_out_ref[LRU_WIDTH:, :])
    x1 = x_ref[0] + gate1 * mixed

    h2 = ((x1 * _rms_scale(x1) * g2_ref[...]) * (1.0 + sc2) + sh2).astype(BF16)

    def conv_cols(c0):
        cols = pl.ds(c0, FF_CHUNK)
        up_ext[V7X_SUBLANES:, cols] = _dot(h2, w_up_ref[:, cols])
        out = cb_ref[:, cols] + cw_ref[FFN_CONV - 1:FFN_CONV, cols] * up_ext[V7X_SUBLANES:, cols]
        for j in range(FFN_CONV - 1):
            back = FFN_CONV - 1 - j
            out = out + cw_ref[j:j + 1, cols] * up_ext[pl.ds(V7X_SUBLANES - back, tm), cols]
        return out

    acc = jnp.zeros((tm, D_MODEL), F32)
    for c in range(D_FF // FF_CHUNK):
        g = conv_cols(c * FF_CHUNK)
        val = conv_cols(D_FF + c * FF_CHUNK)
        act = (g * jax.nn.sigmoid(g) * val).astype(BF16)
        acc = acc + _dot(act, w_down_ref[c * FF_CHUNK:(c + 1) * FF_CHUNK, :])
    up_ext[0:V7X_SUBLANES, :] = up_ext[tm:tm + V7X_SUBLANES, :]

    o_ref[0] = x1 + gate2 * acc


def _mixer_out(x, lru, attn, mod, ga, w_out, g2, w_up, cw, cb, w_down):
    B, S, _ = x.shape
    tm = TM_OUT
    const = lambda shape: pl.BlockSpec(shape, lambda b, s: (0,) * len(shape), pipeline_mode=pl.Buffered(1))
    tok = lambda w: pl.BlockSpec((1, tm, w), lambda b, s: (b, s, 0))
    return pl.pallas_call(
        _mixer_out_kernel,
        grid=(B, S // tm),
        in_specs=[
            tok(D_MODEL),
            tok(LRU_WIDTH),
            tok(ATTN_WIDTH),
            pl.BlockSpec((1, N_MOD, D_MODEL), lambda b, s: (b, 0, 0)),
            const((1, ATTN_WIDTH)),
            const((D_MODEL, D_MODEL)),
            const((1, D_MODEL)),
            const((D_MODEL, 2 * D_FF)),
            const((FFN_CONV, 2 * D_FF)),
            const((1, 2 * D_FF)),
            const((D_FF, D_MODEL)),
        ],
        out_specs=tok(D_MODEL),
        out_shape=jax.ShapeDtypeStruct((B, S, D_MODEL), F32),
        scratch_shapes=[pltpu.VMEM((tm + V7X_SUBLANES, 2 * D_FF), F32)],
        compiler_params=pltpu.CompilerParams(
            dimension_semantics=("arbitrary", "arbitrary"), vmem_limit_bytes=V7X_VMEM_LIMIT_BYTES),
        name="mixer_out",
    )(x, lru, attn, mod, ga, w_out, g2, w_up, cw, cb, w_down)


def _block_diag(w):
    n, d, _ = w.shape
    eye = jnp.eye(n, dtype=w.dtype)
    return jnp.einsum('hij,hg->higj', w, eye).reshape(n * d, n * d)


def _layer(x, c, w_ada, b_ada, norm1_g, w_in, q_norm_g, k_norm_g, lru_conv_w, lru_conv_b, lru_wa, lru_ba,
           lru_wx, lru_bx, lru_lambda, lru_out_g, attn_out_g, w_out, norm2_g, w_up, ffn_conv_w, ffn_conv_b,
           w_down):
    B, S, _ = x.shape
    row = lambda t: t.reshape(1, -1)
    mod = _modulation(c, w_ada, row(b_ada)).reshape(B, N_MOD, D_MODEL)

    head_id = jnp.arange(ATTN_WIDTH) // HEAD_DIM
    gmat = jnp.where(head_id[:, None] == head_id[None, :], 1.0 / HEAD_DIM, 0.0).astype(BF16)
    w_gates = jnp.concatenate([_block_diag(lru_wa), _block_diag(lru_wx)], axis=1).astype(BF16)
    b_gates = jnp.concatenate([lru_ba, lru_bx]).reshape(1, -1)

    q, k, v, bias, lru = _mixer_in(
        x, mod, row(norm1_g), w_in.astype(BF16), row(jnp.tile(q_norm_g, ATTN_HEADS)),
        row(jnp.tile(k_norm_g, ATTN_HEADS)), gmat, lru_conv_w, row(lru_conv_b), w_gates, b_gates,
        row(lru_lambda), row(lru_out_g))
    attn = _moba(q, k, v, bias)
    return _mixer_out(x, lru, attn, mod, row(attn_out_g), w_out.astype(BF16), row(norm2_g), w_up.astype(BF16),
                      ffn_conv_w, row(ffn_conv_b), w_down.astype(BF16))


def kernel(x, c, w_ada, b_ada, norm1_g, w_in, q_norm_g, k_norm_g, lru_conv_w, lru_conv_b, lru_wa, lru_ba, lru_wx, lru_bx, lru_lambda, lru_out_g, attn_out_g, w_out, norm2_g, w_up, ffn_conv_w, ffn_conv_b, w_down):
    depth = w_ada.shape[0]
    for l in range(depth):
        x = _layer(x, c, w_ada[l], b_ada[l], norm1_g[l], w_in[l], q_norm_g[l], k_norm_g[l], lru_conv_w[l],
                   lru_conv_b[l], lru_wa[l], lru_ba[l], lru_wx[l], lru_bx[l], lru_lambda[l], lru_out_g[l],
                   attn_out_g[l], w_out[l], norm2_g[l], w_up[l], ffn_conv_w[l], ffn_conv_b[l], w_down[l])
    return x
```

```python
import functools
import math

import jax
import jax.numpy as jnp
from jax import lax
from jax.experimental import pallas as pl
from jax.experimental.pallas import tpu as pltpu

D_MODEL = 1024
ATTN_HEADS = 8
HEAD_DIM = 64
ATTN_WIDTH = ATTN_HEADS * HEAD_DIM
MOBA_BLOCK = 256
MOBA_TOPK = 3
LRU_WIDTH = D_MODEL - ATTN_WIDTH
LRU_HEADS = 8
LRU_CONV = 4
LRU_C = 8.0
N_IN = 3 * ATTN_WIDTH + 2 * LRU_WIDTH
D_FF = 2816
FFN_CONV = 3
N_MOD = 6
EPS = 1e-6

V7X_LANES = 128
V7X_SUBLANES = 8
V7X_MXU_DIM = 256
V7X_VMEM_LIMIT_BYTES = 56 * 1024 * 1024

HEADS_PER_PAIR = V7X_LANES // HEAD_DIM
N_PAIRS = ATTN_HEADS // HEADS_PER_PAIR
MASK_NEG = -1e30

TM_IN = 512
TM_OUT = 512
KV_CHUNK_BLOCKS = 4
FF_CHUNK = 1408
MOD_CHUNK = 1536

BF16 = jnp.bfloat16
F32 = jnp.float32


def _dot(a, b):
    return jnp.dot(a, b, preferred_element_type=F32)


def _rms_scale(y):
    return lax.rsqrt(jnp.mean(y * y, axis=-1, keepdims=True) + EPS)


def _mod_kernel(c_ref, w_ref, b_ref, o_ref):
    o_ref[...] = _dot(c_ref[...], w_ref[...]) + b_ref[...]


def _modulation(c, w_ada, b_ada):
    B = c.shape[0]
    n = w_ada.shape[1]
    return pl.pallas_call(
        _mod_kernel,
        grid=(n // MOD_CHUNK,),
        in_specs=[
            pl.BlockSpec((B, D_MODEL), lambda j: (0, 0)),
            pl.BlockSpec((D_MODEL, MOD_CHUNK), lambda j: (0, j)),
            pl.BlockSpec((1, MOD_CHUNK), lambda j: (0, j)),
        ],
        out_specs=pl.BlockSpec((B, MOD_CHUNK), lambda j: (0, j)),
        out_shape=jax.ShapeDtypeStruct((B, n), F32),
        compiler_params=pltpu.CompilerParams(
            dimension_semantics=("arbitrary",), vmem_limit_bytes=V7X_VMEM_LIMIT_BYTES),
        name="modulation",
    )(c, w_ada, b_ada)


def _head_rmsnorm(t, gmat, gain):
    sq = t * t
    hi = sq.astype(BF16)
    lo = (sq - hi.astype(F32)).astype(BF16)
    ms = jnp.concatenate(
        [_dot(hi[:, c:c + V7X_MXU_DIM], gmat) + _dot(lo[:, c:c + V7X_MXU_DIM], gmat)
         for c in range(0, t.shape[1], V7X_MXU_DIM)], axis=1)
    return t * lax.rsqrt(ms + EPS) * gain


def _shift_rows(t, d, fill):
    tm = t.shape[0]
    if d % V7X_SUBLANES == 0:
        pad = jnp.full((d, t.shape[1]), fill, t.dtype)
        return jnp.concatenate([pad, t[: tm - d]], axis=0)
    row = lax.broadcasted_iota(jnp.int32, t.shape, 0)
    return jnp.where(row >= d, pltpu.roll(t, d, 0), fill)


SCAN_RUN = V7X_SUBLANES
SCAN_GROUP = SCAN_RUN * V7X_SUBLANES


def _to_run_major(buf, t):
    tm, width = t.shape
    lane_groups = width // V7X_LANES
    for l in range(lane_groups):
        buf[l] = t[:, l * V7X_LANES:(l + 1) * V7X_LANES]
    pieces = []
    for g in range(tm // SCAN_GROUP):
        for s in range(SCAN_RUN):
            rows = pl.ds(g * SCAN_GROUP + s, V7X_SUBLANES, stride=SCAN_RUN)
            pieces.append(jnp.concatenate([buf[l, rows, :] for l in range(lane_groups)], axis=1))
    return jnp.concatenate(pieces, axis=0)


def _from_run_major(buf, t):
    tm, width = t.shape
    lane_groups = width // V7X_LANES
    for g in range(tm // SCAN_GROUP):
        for s in range(SCAN_RUN):
            piece = t[g * SCAN_GROUP + s * V7X_SUBLANES:g * SCAN_GROUP + (s + 1) * V7X_SUBLANES]
            rows = pl.ds(g * SCAN_GROUP + s, V7X_SUBLANES, stride=SCAN_RUN)
            for l in range(lane_groups):
                buf[l, rows, :] = piece[:, l * V7X_LANES:(l + 1) * V7X_LANES]
    return jnp.concatenate([buf[l] for l in range(lane_groups)], axis=1)


def _linear_scan_run_major(a, u, h_in):
    tm = a.shape[0]
    row = lax.broadcasted_iota(jnp.int32, (V7X_SUBLANES, a.shape[1]), 0)
    out = []
    for g in range(tm // SCAN_GROUP):
        h_loc, a_run = [], []
        for s in range(SCAN_RUN):
            rows = slice(g * SCAN_GROUP + s * V7X_SUBLANES, g * SCAN_GROUP + (s + 1) * V7X_SUBLANES)
            h_loc.append(u[rows] if s == 0 else a[rows] * h_loc[-1] + u[rows])
            a_run.append(a[rows] if s == 0 else a[rows] * a_run[-1])
        h_end, a_end = h_loc[-1], a_run[-1]
        d = 1
        while d < V7X_SUBLANES:
            h_end = a_end * _shift_rows(h_end, d, 0.0) + h_end
            a_end = a_end * _shift_rows(a_end, d, 1.0)
            d *= 2
        after = h_end + a_end * h_in
        before = jnp.where(row >= 1, pltpu.roll(after, 1, 0), h_in)
        h_in = after[V7X_SUBLANES - 1:V7X_SUBLANES]
        out += [h_loc[s] + a_run[s] * before for s in range(SCAN_RUN)]
    return jnp.concatenate(out, axis=0), h_in


def _gelu_tanh(t):
    return 0.5 * t * (1.0 + jnp.tanh(math.sqrt(2.0 / math.pi) * (t + 0.044715 * (t * t * t))))


def _mixer_in_kernel(x_ref, mod_ref, g1_ref, w_in_ref, gq_ref, gk_ref, gmat_ref, cw_ref, cb_ref,
                     wg_ref, bg_ref, lam_ref, go_ref,
                     q_ref, k_ref, v_ref, bias_ref, lru_ref,
                     xr_ext, h_carry, km_all, scan_buf):
    @pl.when(pl.program_id(1) == 0)
    def _():
        xr_ext[0:V7X_SUBLANES, :] = jnp.zeros((V7X_SUBLANES, LRU_WIDTH), F32)
        h_carry[...] = jnp.zeros_like(h_carry)
        km_all[...] = jnp.zeros_like(km_all)

    blocks_per_step = x_ref.shape[1] // MOBA_BLOCK
    mod = mod_ref[0]
    gain = g1_ref[...] * (1.0 + mod[1:2, :])
    shift = mod[0:1, :]
    normed = []
    for sub in range(blocks_per_step):
        x = x_ref[0, sub * MOBA_BLOCK:(sub + 1) * MOBA_BLOCK, :]
        normed.append(((x * _rms_scale(x)) * gain + shift).astype(BF16))
    for sub in range(blocks_per_step):
        _attention_side(sub, pl.program_id(1) * blocks_per_step + sub, normed[sub], w_in_ref, gq_ref, gk_ref,
                        gmat_ref, q_ref, k_ref, v_ref, bias_ref, km_all)
    for sub in range(blocks_per_step):
        _lru_side(sub, normed[sub], w_in_ref, cw_ref, cb_ref, wg_ref, bg_ref, lam_ref, go_ref, lru_ref,
                  xr_ext, h_carry, scan_buf)


def _in_proj(hb, w_in_ref, j):
    return _dot(hb, w_in_ref[:, j * ATTN_WIDTH:(j + 1) * ATTN_WIDTH])


def _attention_side(sub, blk_id, hb, w_in_ref, gq_ref, gk_ref, gmat_ref, q_ref, k_ref, v_ref, bias_ref, km_all):
    tm = MOBA_BLOCK
    rows = slice(sub * tm, (sub + 1) * tm)
    proj = functools.partial(_in_proj, hb, w_in_ref)
    gmat = gmat_ref[...]
    qn = _head_rmsnorm(proj(0), gmat, gq_ref[...] * (HEAD_DIM ** -0.5))
    q_t = qn.T.astype(BF16)
    q_ref[0, :, rows] = q_t

    nb = km_all.shape[0]
    km = km_all[...].astype(BF16)
    feat = lax.broadcasted_iota(jnp.int32, (V7X_LANES, 1), 0)
    for head in range(ATTN_HEADS):
        p, h = divmod(head, HEADS_PER_PAIR)
        q_pair = q_t[p * V7X_LANES:(p + 1) * V7X_LANES]
        qh = jnp.where((feat >= h * HEAD_DIM) & (feat < (h + 1) * HEAD_DIM), q_pair, jnp.zeros_like(q_pair))
        gate = _dot(km[:, p * V7X_LANES:(p + 1) * V7X_LANES], qh)
        bias_ref[0, head * nb:(head + 1) * nb, rows] = _select_bias(gate, blk_id).astype(BF16)

    kn = _head_rmsnorm(proj(1), gmat, gk_ref[...])
    k_ref[0, rows, :] = kn.astype(BF16)
    km_all[pl.ds(blk_id, 1), :] = jnp.mean(kn, axis=0, keepdims=True)
    v_ref[0, :, rows] = proj(2).T.astype(BF16)


def _lru_side(sub, hb, w_in_ref, cw_ref, cb_ref, wg_ref, bg_ref, lam_ref, go_ref, lru_ref, xr_ext, h_carry, scan_buf):
    tm = MOBA_BLOCK
    rows = slice(sub * tm, (sub + 1) * tm)
    proj = functools.partial(_in_proj, hb, w_in_ref)

    xr_ext[V7X_SUBLANES:, :] = proj(3)
    xc = cb_ref[...] + cw_ref[LRU_CONV - 1:LRU_CONV, :] * xr_ext[V7X_SUBLANES:, :]
    for j in range(LRU_CONV - 1):
        back = LRU_CONV - 1 - j
        xc = xc + cw_ref[j:j + 1, :] * xr_ext[pl.ds(V7X_SUBLANES - back, tm), :]
    xr_ext[0:V7X_SUBLANES, :] = xr_ext[tm:tm + V7X_SUBLANES, :]

    xc = _to_run_major(scan_buf, xc)
    xcb = xc.astype(BF16)
    pre = [_dot(xcb[:, j * V7X_MXU_DIM:(j + 1) * V7X_MXU_DIM], wg_ref[j]) for j in range(LRU_WIDTH // V7X_MXU_DIM)]
    bg = bg_ref[...]
    r = jax.nn.sigmoid(jnp.concatenate([g[:, :V7X_MXU_DIM] for g in pre], axis=1) + bg[:, :LRU_WIDTH])
    i = jax.nn.sigmoid(jnp.concatenate([g[:, V7X_MXU_DIM:] for g in pre], axis=1) + bg[:, LRU_WIDTH:])
    neg_lam = -lam_ref[...]
    softplus = jnp.maximum(neg_lam, 0.0) + jnp.log1p(jnp.exp(-jnp.abs(neg_lam)))
    log_a = r * ((-LRU_C) * softplus)
    a = jnp.exp(log_a)
    th = jnp.tanh(log_a)
    u = jnp.sqrt(-2.0 * th / (1.0 - th)) * (i * xc)
    hs, h_last = _linear_scan_run_major(a, u, h_carry[0:1, :])
    h_carry[0:1, :] = h_last
    hs = _from_run_major(scan_buf, hs)

    y = hs * _gelu_tanh(proj(4))
    lru_ref[0, rows, :] = (y * _rms_scale(y) * go_ref[...]).astype(BF16)


def _mixer_in(x, mod, g1, w_in, gq, gk, gmat, cw, cb, wg, bg, lam, go):
    B, S, _ = x.shape
    tm = TM_IN
    assert tm % MOBA_BLOCK == 0, "a grid step covers whole MoBA blocks: the block choice reads the means of earlier blocks"
    nb = S // MOBA_BLOCK
    const = lambda shape: pl.BlockSpec(shape, lambda b, s: (0,) * len(shape))
    tok = lambda w: pl.BlockSpec((1, tm, w), lambda b, s: (b, s, 0))
    feat_major = pl.BlockSpec((1, ATTN_WIDTH, tm), lambda b, s: (b, 0, s))
    return pl.pallas_call(
        _mixer_in_kernel,
        grid=(B, S // tm),
        in_specs=[
            tok(D_MODEL),
            pl.BlockSpec((1, N_MOD, D_MODEL), lambda b, s: (b, 0, 0)),
            const((1, D_MODEL)),
            const((D_MODEL, N_IN)),
            const((1, ATTN_WIDTH)),
            const((1, ATTN_WIDTH)),
            const((V7X_MXU_DIM, V7X_MXU_DIM)),
            const((LRU_CONV, LRU_WIDTH)),
            const((1, LRU_WIDTH)),
            const((LRU_WIDTH // V7X_MXU_DIM, V7X_MXU_DIM, 2 * V7X_MXU_DIM)),
            const((1, 2 * LRU_WIDTH)),
            const((1, LRU_WIDTH)),
            const((1, LRU_WIDTH)),
        ],
        out_specs=[
            feat_major,
            tok(ATTN_WIDTH),
            feat_major,
            pl.BlockSpec((1, ATTN_HEADS * nb, tm), lambda b, s: (b, 0, s)),
            tok(LRU_WIDTH),
        ],
        out_shape=[
            jax.ShapeDtypeStruct((B, ATTN_WIDTH, S), BF16),
            jax.ShapeDtypeStruct((B, S, ATTN_WIDTH), BF16),
            jax.ShapeDtypeStruct((B, ATTN_WIDTH, S), BF16),
            jax.ShapeDtypeStruct((B, ATTN_HEADS * nb, S), BF16),
            jax.ShapeDtypeStruct((B, S, LRU_WIDTH), BF16),
        ],
        scratch_shapes=[
            pltpu.VMEM((MOBA_BLOCK + V7X_SUBLANES, LRU_WIDTH), F32),
            pltpu.VMEM((V7X_SUBLANES, LRU_WIDTH), F32),
            pltpu.VMEM((nb, ATTN_WIDTH), F32),
            pltpu.VMEM((LRU_WIDTH // V7X_LANES, MOBA_BLOCK, V7X_LANES), F32),
        ],
        compiler_params=pltpu.CompilerParams(
            dimension_semantics=("arbitrary", "arbitrary"), vmem_limit_bytes=V7X_VMEM_LIMIT_BYTES),
        name="mixer_in",
    )(x, mod, g1, w_in, gq, gk, gmat, cw, cb, wg, bg, lam, go)


def _select_bias(gate, n_past):
    nb = gate.shape[0]
    blk = lax.broadcasted_iota(jnp.int32, gate.shape, 0)
    neg_inf = jnp.float32(-jnp.inf)
    g = jnp.where(blk < n_past, gate, neg_inf)
    bias = jnp.full(gate.shape, MASK_NEG, F32)
    for _ in range(MOBA_TOPK):
        mx = jnp.max(g, axis=0, keepdims=True)
        cand = jnp.where((g == mx) & (mx > neg_inf), blk, nb)
        pick = blk == jnp.min(cand, axis=0, keepdims=True)
        bias = jnp.where(pick, 0.0, bias)
        g = jnp.where(pick, neg_inf, g)
    return bias


def _moba_kernel(q_ref, k_ref, v_ref, bias_ref, blk_ref, o_ref, s_even, s_odd, mx_even, mx_odd):
    def tile(i, _):
        _moba_tile(i, q_ref, k_ref, v_ref, bias_ref, blk_ref, o_ref, s_even, s_odd, mx_even, mx_odd)
        return 0

    lax.fori_loop(0, q_ref.shape[2] // MOBA_BLOCK, tile, 0)


def _moba_tile(i, q_ref, k_ref, v_ref, bias_ref, blk_ref, o_ref, s_even, s_odd, mx_even, mx_odd):
    tq = MOBA_BLOCK
    nb = bias_ref.shape[1] // HEADS_PER_PAIR
    feat = lax.broadcasted_iota(jnp.int32, (V7X_LANES, 1), 0)
    start = pl.multiple_of(i * MOBA_BLOCK, MOBA_BLOCK)
    q_t = q_ref[0, :, pl.ds(start, tq)]

    def v_ext(h, start, n):
        vh = v_ref[0, h * HEAD_DIM:(h + 1) * HEAD_DIM, pl.ds(start, n)]
        return jnp.concatenate([vh, jnp.ones((2 * V7X_SUBLANES, n), BF16)], axis=0)

    kd = k_ref[0, pl.ds(start, MOBA_BLOCK), :]
    key_pos = lax.broadcasted_iota(jnp.int32, (MOBA_BLOCK, tq), 0)
    qry_pos = lax.broadcasted_iota(jnp.int32, (MOBA_BLOCK, tq), 1)
    feat_pad = jnp.zeros((V7X_LANES - nb, tq), BF16)

    q_heads = [jnp.where((feat >= h * HEAD_DIM) & (feat < (h + 1) * HEAD_DIM), q_t, jnp.zeros_like(q_t))
               for h in range(HEADS_PER_PAIR)]
    qx = jnp.concatenate(
        [jnp.concatenate([q_heads[h], bias_ref[0, h * nb:(h + 1) * nb, pl.ds(start, tq)], feat_pad], axis=0)
         for h in range(HEADS_PER_PAIR)], axis=1)

    ck = KV_CHUNK_BLOCKS * MOBA_BLOCK
    n_chunks = (i + KV_CHUNK_BLOCKS - 1) // KV_CHUNK_BLOCKS

    def block_scores(c, t):
        ks = pl.multiple_of(c * ck + t * MOBA_BLOCK, MOBA_BLOCK)
        kx = jnp.concatenate([k_ref[0, pl.ds(ks, MOBA_BLOCK), :], blk_ref[pl.ds(ks, MOBA_BLOCK), :]], axis=1)
        return _dot(kx, qx)

    def run_step(c, cur, nxt, carry):
        s_ref, mx_ref = cur
        m = [carry[2 * h] for h in range(HEADS_PER_PAIR)]
        acc = [carry[2 * h + 1] for h in range(HEADS_PER_PAIR)]
        for t in range(KV_CHUNK_BLOCKS):
            rows = slice(t * MOBA_BLOCK, (t + 1) * MOBA_BLOCK)
            if nxt is not None:
                s_nxt = block_scores(c + 1, t)
                nxt[0][rows, :] = s_nxt
                nxt[1][t:t + 1, :] = jnp.max(s_nxt, axis=0, keepdims=True)
            ks = pl.multiple_of(c * ck + t * MOBA_BLOCK, MOBA_BLOCK)
            for h in range(HEADS_PER_PAIR):
                cols = slice(h * tq, (h + 1) * tq)
                m_new = jnp.maximum(m[h], mx_ref[t:t + 1, cols])
                p = jnp.exp(s_ref[rows, cols] - m_new).astype(BF16)
                acc[h] = jnp.exp(m[h] - m_new) * acc[h] + _dot(v_ext(h, ks, MOBA_BLOCK), p)
                m[h] = m_new
        return (m[0], acc[0], m[1], acc[1])

    even, odd = (s_even, mx_even), (s_odd, mx_odd)

    s_own = [jnp.where(key_pos <= qry_pos, _dot(kd, q_heads[h]), MASK_NEG) for h in range(HEADS_PER_PAIR)]
    for t in range(KV_CHUNK_BLOCKS):
        s_first = block_scores(0, t)
        s_even[t * MOBA_BLOCK:(t + 1) * MOBA_BLOCK, :] = s_first
        mx_even[t:t + 1, :] = jnp.max(s_first, axis=0, keepdims=True)
    carry = []
    for h in range(HEADS_PER_PAIR):
        m = jnp.max(s_own[h], axis=0, keepdims=True)
        p = jnp.exp(s_own[h] - m).astype(BF16)
        carry += [m, _dot(v_ext(h, start, MOBA_BLOCK), p)]

    def pair(k, carry):
        carry = run_step(2 * k, even, odd, carry)
        return run_step(2 * k + 1, odd, even, carry)

    carry = lax.fori_loop(0, jnp.maximum(n_chunks - 1, 0) // 2, pair, tuple(carry))

    def last_two(carry):
        carry = run_step(n_chunks - 2, even, odd, carry)
        return run_step(n_chunks - 1, odd, None, carry)

    def finish(carry):
        heads = []
        for h in range(HEADS_PER_PAIR):
            acc = carry[2 * h + 1]
            heads.append(acc[0:HEAD_DIM] / acc[HEAD_DIM:HEAD_DIM + 1])
        o_ref[0, pl.ds(start, tq), :] = jnp.concatenate(heads, axis=0).T
        return 0

    lax.cond(
        n_chunks % 2 == 1,
        lambda: finish(run_step(n_chunks - 1, even, None, carry)),
        lambda: lax.cond(n_chunks > 0, lambda: finish(last_two(carry)), lambda: finish(carry)))


def _moba(q, k, v, bias):
    B, S, _ = k.shape
    nb = S // MOBA_BLOCK
    tq = MOBA_BLOCK
    ck = KV_CHUNK_BLOCKS * MOBA_BLOCK
    block_onehot = (jnp.arange(S)[:, None] // MOBA_BLOCK == jnp.arange(V7X_LANES)[None, :]).astype(BF16)
    feat_major = pl.BlockSpec((1, V7X_LANES, S), lambda b, p: (b, p, 0))
    return pl.pallas_call(
        _moba_kernel,
        grid=(B, N_PAIRS),
        in_specs=[
            feat_major,
            pl.BlockSpec((1, S, V7X_LANES), lambda b, p: (b, 0, p)),
            feat_major,
            pl.BlockSpec((1, HEADS_PER_PAIR * nb, S), lambda b, p: (b, p, 0)),
            pl.BlockSpec((S, V7X_LANES), lambda b, p: (0, 0), pipeline_mode=pl.Buffered(1)),
        ],
        out_specs=pl.BlockSpec((1, S, V7X_LANES), lambda b, p: (b, 0, p)),
        out_shape=jax.ShapeDtypeStruct((B, S, ATTN_WIDTH), F32),
        scratch_shapes=[
            pltpu.VMEM((ck, HEADS_PER_PAIR * tq), F32),
            pltpu.VMEM((ck, HEADS_PER_PAIR * tq), F32),
            pltpu.VMEM((V7X_SUBLANES, HEADS_PER_PAIR * tq), F32),
            pltpu.VMEM((V7X_SUBLANES, HEADS_PER_PAIR * tq), F32),
        ],
        compiler_params=pltpu.CompilerParams(
            dimension_semantics=("arbitrary", "arbitrary"),
            vmem_limit_bytes=V7X_VMEM_LIMIT_BYTES),
        name="moba",
    )(q, k, v, bias, block_onehot)


def _mixer_out_kernel(x_ref, lru_ref, attn_ref, mod_ref, ga_ref, w_out_ref, g2_ref, w_up_ref, cw_ref, cb_ref,
                      w_down_ref, o_ref, up_ext):
    tm = x_ref.shape[1]

    @pl.when(pl.program_id(1) == 0)
    def _():
        up_ext[0:V7X_SUBLANES, :] = jnp.zeros((V7X_SUBLANES, 2 * D_FF), F32)

    mod = mod_ref[0]
    gate1, sh2, sc2, gate2 = mod[2:3, :], mod[3:4, :], mod[4:5, :], mod[5:6, :]

    attn = attn_ref[0]
    attn_n = (attn * _rms_scale(attn) * ga_ref[...]).astype(BF16)
    mixed = _dot(lru_ref[0], w_out_ref[0:LRU_WIDTH, :]) + _dot(attn_n, w_out_ref[LRU_WIDTH:, :])
    x1 = x_ref[0] + gate1 * mixed

    h2 = ((x1 * _rms_scale(x1) * g2_ref[...]) * (1.0 + sc2) + sh2).astype(BF16)

    def conv_cols(c0):
        cols = pl.ds(c0, FF_CHUNK)
        up_ext[V7X_SUBLANES:, cols] = _dot(h2, w_up_ref[:, cols])
        out = cb_ref[:, cols] + cw_ref[FFN_CONV - 1:FFN_CONV, cols] * up_ext[V7X_SUBLANES:, cols]
        for j in range(FFN_CONV - 1):
            back = FFN_CONV - 1 - j
            out = out + cw_ref[j:j + 1, cols] * up_ext[pl.ds(V7X_SUBLANES - back, tm), cols]
        return out

    acc = jnp.zeros((tm, D_MODEL), F32)
    for c in range(D_FF // FF_CHUNK):
        g = conv_cols(c * FF_CHUNK)
        val = conv_cols(D_FF + c * FF_CHUNK)
        act = (g * jax.nn.sigmoid(g) * val).astype(BF16)
        acc = acc + _dot(act, w_down_ref[c * FF_CHUNK:(c + 1) * FF_CHUNK, :])
    up_ext[0:V7X_SUBLANES, :] = up_ext[tm:tm + V7X_SUBLANES, :]

    o_ref[0] = x1 + gate2 * acc


def _mixer_out(x, lru, attn, mod, ga, w_out, g2, w_up, cw, cb, w_down):
    B, S, _ = x.shape
    tm = TM_OUT
    const = lambda shape: pl.BlockSpec(shape, lambda b, s: (0,) * len(shape), pipeline_mode=pl.Buffered(1))
    tok = lambda w: pl.BlockSpec((1, tm, w), lambda b, s: (b, s, 0))
    return pl.pallas_call(
        _mixer_out_kernel,
        grid=(B, S // tm),
        in_specs=[
            tok(D_MODEL),
            tok(LRU_WIDTH),
            tok(ATTN_WIDTH),
            pl.BlockSpec((1, N_MOD, D_MODEL), lambda b, s: (b, 0, 0)),
            const((1, ATTN_WIDTH)),
            const((D_MODEL, D_MODEL)),
            const((1, D_MODEL)),
            const((D_MODEL, 2 * D_FF)),
            const((FFN_CONV, 2 * D_FF)),
            const((1, 2 * D_FF)),
            const((D_FF, D_MODEL)),
        ],
        out_specs=tok(D_MODEL),
        out_shape=jax.ShapeDtypeStruct((B, S, D_MODEL), F32),
        scratch_shapes=[pltpu.VMEM((tm + V7X_SUBLANES, 2 * D_FF), F32)],
        compiler_params=pltpu.CompilerParams(
            dimension_semantics=("arbitrary", "arbitrary"), vmem_limit_bytes=V7X_VMEM_LIMIT_BYTES),
        name="mixer_out",
    )(x, lru, attn, mod, ga, w_out, g2, w_up, cw, cb, w_down)


def _block_diag(w):
    n, d, _ = w.shape
    eye = jnp.eye(n, dtype=w.dtype)
    return jnp.einsum('hij,hg->higj', w, eye).reshape(n * d, n * d)


def _layer(x, c, w_ada, b_ada, norm1_g, w_in, q_norm_g, k_norm_g, lru_conv_w, lru_conv_b, lru_wa, lru_ba,
           lru_wx, lru_bx, lru_lambda, lru_out_g, attn_out_g, w_out, norm2_g, w_up, ffn_conv_w, ffn_conv_b,
           w_down):
    B, S, _ = x.shape
    row = lambda t: t.reshape(1, -1)
    mod = _modulation(c, w_ada, row(b_ada)).reshape(B, N_MOD, D_MODEL)

    head_id = jnp.arange(V7X_MXU_DIM) // HEAD_DIM
    gmat = jnp.where(head_id[:, None] == head_id[None, :], 1.0 / HEAD_DIM, 0.0).astype(BF16)
    heads_per_tile = V7X_MXU_DIM // (LRU_WIDTH // LRU_HEADS)
    w_gates = jnp.stack([
        jnp.concatenate([_block_diag(w[j * heads_per_tile:(j + 1) * heads_per_tile]) for w in (lru_wa, lru_wx)], axis=1)
        for j in range(LRU_HEADS // heads_per_tile)]).astype(BF16)
    b_gates = jnp.concatenate([lru_ba, lru_bx]).reshape(1, -1)

    q, k, v, bias, lru = _mixer_in(
        x, mod, row(norm1_g), w_in.astype(BF16), row(jnp.tile(q_norm_g, ATTN_HEADS)),
        row(jnp.tile(k_norm_g, ATTN_HEADS)), gmat, lru_conv_w, row(lru_conv_b), w_gates, b_gates,
        row(lru_lambda), row(lru_out_g))
    attn = _moba(q, k, v, bias)
    return _mixer_out(x, lru, attn, mod, row(attn_out_g), w_out.astype(BF16), row(norm2_g), w_up.astype(BF16),
                      ffn_conv_w, row(ffn_conv_b), w_down.astype(BF16))


def kernel(x, c, w_ada, b_ada, norm1_g, w_in, q_norm_g, k_norm_g, lru_conv_w, lru_conv_b, lru_wa, lru_ba, lru_wx, lru_bx, lru_lambda, lru_out_g, attn_out_g, w_out, norm2_g, w_up, ffn_conv_w, ffn_conv_b, w_down):
    depth = w_ada.shape[0]
    for l in range(depth):
        x = _layer(x, c, w_ada[l], b_ada[l], norm1_g[l], w_in[l], q_norm_g[l], k_norm_g[l], lru_conv_w[l],
                   lru_conv_b[l], lru_wa[l], lru_ba[l], lru_wx[l], lru_bx[l], lru_lambda[l], lru_out_g[l],
                   attn_out_g[l], w_out[l], norm2_g[l], w_up[l], ffn_conv_w[l], ffn_conv_b[l], w_down[l])
    return x
```

```python
import functools
import math

import jax
import jax.numpy as jnp
from jax import lax
from jax.experimental import pallas as pl
from jax.experimental.pallas import tpu as pltpu

D_MODEL = 1024
ATTN_HEADS = 8
HEAD_DIM = 64
ATTN_WIDTH = ATTN_HEADS * HEAD_DIM
MOBA_BLOCK = 256
MOBA_TOPK = 3
LRU_WIDTH = D_MODEL - ATTN_WIDTH
LRU_HEADS = 8
LRU_CONV = 4
LRU_C = 8.0
N_IN = 3 * ATTN_WIDTH + 2 * LRU_WIDTH
D_FF = 2816
FFN_CONV = 3
N_MOD = 6
EPS = 1e-6

V7X_LANES = 128
V7X_SUBLANES = 8
V7X_MXU_DIM = 256
V7X_VMEM_LIMIT_BYTES = 56 * 1024 * 1024

HEADS_PER_PAIR = V7X_LANES // HEAD_DIM
N_PAIRS = ATTN_HEADS // HEADS_PER_PAIR
MASK_NEG = -1e30

TM_IN = 512
TM_OUT = 512
KV_CHUNK_BLOCKS = 4
FF_CHUNK = 1408
MOD_CHUNK = 1536

BF16 = jnp.bfloat16
F32 = jnp.float32


def _dot(a, b):
    return jnp.dot(a, b, preferred_element_type=F32)


def _rms_scale(y):
    return lax.rsqrt(jnp.mean(y * y, axis=-1, keepdims=True) + EPS)


def _mod_kernel(c_ref, w_ref, b_ref, o_ref):
    o_ref[...] = _dot(c_ref[...], w_ref[...]) + b_ref[...]


def _modulation(c, w_ada, b_ada):
    B = c.shape[0]
    n = w_ada.shape[1]
    return pl.pallas_call(
        _mod_kernel,
        grid=(n // MOD_CHUNK,),
        in_specs=[
            pl.BlockSpec((B, D_MODEL), lambda j: (0, 0)),
            pl.BlockSpec((D_MODEL, MOD_CHUNK), lambda j: (0, j)),
            pl.BlockSpec((1, MOD_CHUNK), lambda j: (0, j)),
        ],
        out_specs=pl.BlockSpec((B, MOD_CHUNK), lambda j: (0, j)),
        out_shape=jax.ShapeDtypeStruct((B, n), F32),
        compiler_params=pltpu.CompilerParams(
            dimension_semantics=("arbitrary",), vmem_limit_bytes=V7X_VMEM_LIMIT_BYTES),
        name="modulation",
    )(c, w_ada, b_ada)


def _head_rmsnorm(t, gmat, gain):
    sq = t * t
    hi = sq.astype(BF16)
    lo = (sq - hi.astype(F32)).astype(BF16)
    ms = jnp.concatenate(
        [_dot(hi[:, c:c + V7X_MXU_DIM], gmat) + _dot(lo[:, c:c + V7X_MXU_DIM], gmat)
         for c in range(0, t.shape[1], V7X_MXU_DIM)], axis=1)
    return t * lax.rsqrt(ms + EPS) * gain


def _shift_rows(t, d, fill):
    tm = t.shape[0]
    if d % V7X_SUBLANES == 0:
        pad = jnp.full((d, t.shape[1]), fill, t.dtype)
        return jnp.concatenate([pad, t[: tm - d]], axis=0)
    row = lax.broadcasted_iota(jnp.int32, t.shape, 0)
    return jnp.where(row >= d, pltpu.roll(t, d, 0), fill)


SCAN_RUN = V7X_SUBLANES
SCAN_GROUP = SCAN_RUN * V7X_SUBLANES


def _to_run_major(buf, t):
    tm, width = t.shape
    lane_groups = width // V7X_LANES
    for l in range(lane_groups):
        buf[l] = t[:, l * V7X_LANES:(l + 1) * V7X_LANES]
    pieces = []
    for g in range(tm // SCAN_GROUP):
        for s in range(SCAN_RUN):
            rows = pl.ds(g * SCAN_GROUP + s, V7X_SUBLANES, stride=SCAN_RUN)
            pieces.append(jnp.concatenate([buf[l, rows, :] for l in range(lane_groups)], axis=1))
    return jnp.concatenate(pieces, axis=0)


def _from_run_major(buf, t):
    tm, width = t.shape
    lane_groups = width // V7X_LANES
    for g in range(tm // SCAN_GROUP):
        for s in range(SCAN_RUN):
            piece = t[g * SCAN_GROUP + s * V7X_SUBLANES:g * SCAN_GROUP + (s + 1) * V7X_SUBLANES]
            rows = pl.ds(g * SCAN_GROUP + s, V7X_SUBLANES, stride=SCAN_RUN)
            for l in range(lane_groups):
                buf[l, rows, :] = piece[:, l * V7X_LANES:(l + 1) * V7X_LANES]
    return jnp.concatenate([buf[l] for l in range(lane_groups)], axis=1)


def _causal_conv_run_major(xp, w_ref, b_ref, carry_ref):
    taps = w_ref.shape[0]
    tm, width = xp.shape
    row = lax.broadcasted_iota(jnp.int32, (V7X_SUBLANES, width), 0)
    wrapped = range(SCAN_RUN - (taps - 1), SCAN_RUN)
    above = {s: carry_ref[s:s + 1, :] for s in wrapped}
    bias = b_ref[...]
    out = []
    for g in range(tm // SCAN_GROUP):
        def piece(s):
            return xp[g * SCAN_GROUP + s * V7X_SUBLANES:g * SCAN_GROUP + (s + 1) * V7X_SUBLANES]

        prev_run = {}
        for s in wrapped:
            prev_run[s] = jnp.where(row >= 1, pltpu.roll(piece(s), 1, 0), above[s])
            above[s] = piece(s)[V7X_SUBLANES - 1:V7X_SUBLANES]
        for s in range(SCAN_RUN):
            acc = bias + w_ref[taps - 1:taps, :] * piece(s)
            for d in range(1, taps):
                src = piece(s - d) if s >= d else prev_run[s - d + SCAN_RUN]
                acc = acc + w_ref[taps - 1 - d:taps - d, :] * src
            out.append(acc)
    for s in wrapped:
        carry_ref[s:s + 1, :] = above[s]
    return jnp.concatenate(out, axis=0)


def _linear_scan_run_major(a, u, h_in):
    tm = a.shape[0]
    row = lax.broadcasted_iota(jnp.int32, (V7X_SUBLANES, a.shape[1]), 0)
    out = []
    for g in range(tm // SCAN_GROUP):
        h_loc, a_run = [], []
        for s in range(SCAN_RUN):
            rows = slice(g * SCAN_GROUP + s * V7X_SUBLANES, g * SCAN_GROUP + (s + 1) * V7X_SUBLANES)
            h_loc.append(u[rows] if s == 0 else a[rows] * h_loc[-1] + u[rows])
            a_run.append(a[rows] if s == 0 else a[rows] * a_run[-1])
        h_end, a_end = h_loc[-1], a_run[-1]
        d = 1
        while d < V7X_SUBLANES:
            h_end = a_end * _shift_rows(h_end, d, 0.0) + h_end
            a_end = a_end * _shift_rows(a_end, d, 1.0)
            d *= 2
        after = h_end + a_end * h_in
        before = jnp.where(row >= 1, pltpu.roll(after, 1, 0), h_in)
        h_in = after[V7X_SUBLANES - 1:V7X_SUBLANES]
        out += [h_loc[s] + a_run[s] * before for s in range(SCAN_RUN)]
    return jnp.concatenate(out, axis=0), h_in


def _gelu_tanh(t):
    return 0.5 * t * (1.0 + jnp.tanh(math.sqrt(2.0 / math.pi) * (t + 0.044715 * (t * t * t))))


def _mixer_in_kernel(x_ref, mod_ref, g1_ref, w_in_ref, gq_ref, gk_ref, gmat_ref, cw_ref, cb_ref,
                     wg_ref, bg_ref, lam_ref, go_ref,
                     q_ref, k_ref, v_ref, bias_ref, lru_ref,
                     conv_carry, h_carry, km_all, scan_buf):
    @pl.when(pl.program_id(1) == 0)
    def _():
        conv_carry[...] = jnp.zeros_like(conv_carry)
        h_carry[...] = jnp.zeros_like(h_carry)
        km_all[...] = jnp.zeros_like(km_all)

    blocks_per_step = x_ref.shape[1] // MOBA_BLOCK
    mod = mod_ref[0]
    gain = g1_ref[...] * (1.0 + mod[1:2, :])
    shift = mod[0:1, :]
    normed = []
    for sub in range(blocks_per_step):
        x = x_ref[0, sub * MOBA_BLOCK:(sub + 1) * MOBA_BLOCK, :]
        normed.append(((x * _rms_scale(x)) * gain + shift).astype(BF16))
    for sub in range(blocks_per_step):
        _attention_side(sub, pl.program_id(1) * blocks_per_step + sub, normed[sub], w_in_ref, gq_ref, gk_ref,
                        gmat_ref, q_ref, k_ref, v_ref, bias_ref, km_all)
    for sub in range(blocks_per_step):
        _lru_side(sub, normed[sub], w_in_ref, cw_ref, cb_ref, wg_ref, bg_ref, lam_ref, go_ref, lru_ref,
                  conv_carry, h_carry, scan_buf)


def _in_proj(hb, w_in_ref, j):
    return _dot(hb, w_in_ref[:, j * ATTN_WIDTH:(j + 1) * ATTN_WIDTH])


def _attention_side(sub, blk_id, hb, w_in_ref, gq_ref, gk_ref, gmat_ref, q_ref, k_ref, v_ref, bias_ref, km_all):
    tm = MOBA_BLOCK
    rows = slice(sub * tm, (sub + 1) * tm)
    proj = functools.partial(_in_proj, hb, w_in_ref)
    gmat = gmat_ref[...]
    qn = _head_rmsnorm(proj(0), gmat, gq_ref[...] * (HEAD_DIM ** -0.5))
    q_t = qn.T.astype(BF16)
    q_ref[0, :, rows] = q_t

    nb = km_all.shape[0]
    km = km_all[...].astype(BF16)
    feat = lax.broadcasted_iota(jnp.int32, (V7X_LANES, 1), 0)
    for head in range(ATTN_HEADS):
        p, h = divmod(head, HEADS_PER_PAIR)
        q_pair = q_t[p * V7X_LANES:(p + 1) * V7X_LANES]
        qh = jnp.where((feat >= h * HEAD_DIM) & (feat < (h + 1) * HEAD_DIM), q_pair, jnp.zeros_like(q_pair))
        gate = _dot(km[:, p * V7X_LANES:(p + 1) * V7X_LANES], qh)
        bias_ref[0, head * nb:(head + 1) * nb, rows] = _select_bias(gate, blk_id).astype(BF16)

    kn = _head_rmsnorm(proj(1), gmat, gk_ref[...])
    k_ref[0, rows, :] = kn.astype(BF16)
    km_all[pl.ds(blk_id, 1), :] = jnp.mean(kn, axis=0, keepdims=True)
    v_ref[0, :, rows] = proj(2).T.astype(BF16)


def _lru_side(sub, hb, w_in_ref, cw_ref, cb_ref, wg_ref, bg_ref, lam_ref, go_ref, lru_ref, conv_carry, h_carry,
              scan_buf):
    tm = MOBA_BLOCK
    rows = slice(sub * tm, (sub + 1) * tm)
    proj = functools.partial(_in_proj, hb, w_in_ref)

    xc = _causal_conv_run_major(_to_run_major(scan_buf, proj(3)), cw_ref, cb_ref, conv_carry)
    xcb = xc.astype(BF16)
    pre = [_dot(xcb[:, j * V7X_MXU_DIM:(j + 1) * V7X_MXU_DIM], wg_ref[j]) for j in range(LRU_WIDTH // V7X_MXU_DIM)]
    bg = bg_ref[...]
    r = jax.nn.sigmoid(jnp.concatenate([g[:, :V7X_MXU_DIM] for g in pre], axis=1) + bg[:, :LRU_WIDTH])
    i = jax.nn.sigmoid(jnp.concatenate([g[:, V7X_MXU_DIM:] for g in pre], axis=1) + bg[:, LRU_WIDTH:])
    neg_lam = -lam_ref[...]
    softplus = jnp.maximum(neg_lam, 0.0) + jnp.log1p(jnp.exp(-jnp.abs(neg_lam)))
    log_a = r * ((-LRU_C) * softplus)
    a = jnp.exp(log_a)
    th = jnp.tanh(log_a)
    u = jnp.sqrt(-2.0 * th / (1.0 - th)) * (i * xc)
    hs, h_last = _linear_scan_run_major(a, u, h_carry[0:1, :])
    h_carry[0:1, :] = h_last
    hs = _from_run_major(scan_buf, hs)

    y = hs * _gelu_tanh(proj(4))
    lru_ref[0, rows, :] = (y * _rms_scale(y) * go_ref[...]).astype(BF16)


def _mixer_in(x, mod, g1, w_in, gq, gk, gmat, cw, cb, wg, bg, lam, go):
    B, S, _ = x.shape
    tm = TM_IN
    assert tm % MOBA_BLOCK == 0, "a grid step covers whole MoBA blocks: the block choice reads the means of earlier blocks"
    nb = S // MOBA_BLOCK
    const = lambda shape: pl.BlockSpec(shape, lambda b, s: (0,) * len(shape))
    tok = lambda w: pl.BlockSpec((1, tm, w), lambda b, s: (b, s, 0))
    feat_major = pl.BlockSpec((1, ATTN_WIDTH, tm), lambda b, s: (b, 0, s))
    return pl.pallas_call(
        _mixer_in_kernel,
        grid=(B, S // tm),
        in_specs=[
            tok(D_MODEL),
            pl.BlockSpec((1, N_MOD, D_MODEL), lambda b, s: (b, 0, 0)),
            const((1, D_MODEL)),
            const((D_MODEL, N_IN)),
            const((1, ATTN_WIDTH)),
            const((1, ATTN_WIDTH)),
            const((V7X_MXU_DIM, V7X_MXU_DIM)),
            const((LRU_CONV, LRU_WIDTH)),
            const((1, LRU_WIDTH)),
            const((LRU_WIDTH // V7X_MXU_DIM, V7X_MXU_DIM, 2 * V7X_MXU_DIM)),
            const((1, 2 * LRU_WIDTH)),
            const((1, LRU_WIDTH)),
            const((1, LRU_WIDTH)),
        ],
        out_specs=[
            feat_major,
            tok(ATTN_WIDTH),
            feat_major,
            pl.BlockSpec((1, ATTN_HEADS * nb, tm), lambda b, s: (b, 0, s)),
            tok(LRU_WIDTH),
        ],
        out_shape=[
            jax.ShapeDtypeStruct((B, ATTN_WIDTH, S), BF16),
            jax.ShapeDtypeStruct((B, S, ATTN_WIDTH), BF16),
            jax.ShapeDtypeStruct((B, ATTN_WIDTH, S), BF16),
            jax.ShapeDtypeStruct((B, ATTN_HEADS * nb, S), BF16),
            jax.ShapeDtypeStruct((B, S, LRU_WIDTH), BF16),
        ],
        scratch_shapes=[
            pltpu.VMEM((V7X_SUBLANES, LRU_WIDTH), F32),
            pltpu.VMEM((V7X_SUBLANES, LRU_WIDTH), F32),
            pltpu.VMEM((nb, ATTN_WIDTH), F32),
            pltpu.VMEM((LRU_WIDTH // V7X_LANES, MOBA_BLOCK, V7X_LANES), F32),
        ],
        compiler_params=pltpu.CompilerParams(
            dimension_semantics=("arbitrary", "arbitrary"), vmem_limit_bytes=V7X_VMEM_LIMIT_BYTES),
        name="mixer_in",
    )(x, mod, g1, w_in, gq, gk, gmat, cw, cb, wg, bg, lam, go)


def _select_bias(gate, n_past):
    nb = gate.shape[0]
    blk = lax.broadcasted_iota(jnp.int32, gate.shape, 0)
    neg_inf = jnp.float32(-jnp.inf)
    g = jnp.where(blk < n_past, gate, neg_inf)
    bias = jnp.full(gate.shape, MASK_NEG, F32)
    for _ in range(MOBA_TOPK):
        mx = jnp.max(g, axis=0, keepdims=True)
        cand = jnp.where((g == mx) & (mx > neg_inf), blk, nb)
        pick = blk == jnp.min(cand, axis=0, keepdims=True)
        bias = jnp.where(pick, 0.0, bias)
        g = jnp.where(pick, neg_inf, g)
    return bias


def _moba_kernel(q_ref, k_ref, v_ref, bias_ref, blk_ref, o_ref, s_even, s_odd, mx_even, mx_odd):
    def tile(i, _):
        _moba_tile(i, q_ref, k_ref, v_ref, bias_ref, blk_ref, o_ref, s_even, s_odd, mx_even, mx_odd)
        return 0

    lax.fori_loop(0, q_ref.shape[2] // MOBA_BLOCK, tile, 0)


def _moba_tile(i, q_ref, k_ref, v_ref, bias_ref, blk_ref, o_ref, s_even, s_odd, mx_even, mx_odd):
    tq = MOBA_BLOCK
    nb = bias_ref.shape[1] // HEADS_PER_PAIR
    feat = lax.broadcasted_iota(jnp.int32, (V7X_LANES, 1), 0)
    start = pl.multiple_of(i * MOBA_BLOCK, MOBA_BLOCK)
    q_t = q_ref[0, :, pl.ds(start, tq)]

    def v_ext(h, start, n):
        vh = v_ref[0, h * HEAD_DIM:(h + 1) * HEAD_DIM, pl.ds(start, n)]
        return jnp.concatenate([vh, jnp.ones((2 * V7X_SUBLANES, n), BF16)], axis=0)

    kd = k_ref[0, pl.ds(start, MOBA_BLOCK), :]
    key_pos = lax.broadcasted_iota(jnp.int32, (MOBA_BLOCK, tq), 0)
    qry_pos = lax.broadcasted_iota(jnp.int32, (MOBA_BLOCK, tq), 1)
    feat_pad = jnp.zeros((V7X_LANES - nb, tq), BF16)

    q_heads = [jnp.where((feat >= h * HEAD_DIM) & (feat < (h + 1) * HEAD_DIM), q_t, jnp.zeros_like(q_t))
               for h in range(HEADS_PER_PAIR)]
    qx = jnp.concatenate(
        [jnp.concatenate([q_heads[h], bias_ref[0, h * nb:(h + 1) * nb, pl.ds(start, tq)], feat_pad], axis=0)
         for h in range(HEADS_PER_PAIR)], axis=1)

    ck = KV_CHUNK_BLOCKS * MOBA_BLOCK
    n_chunks = (i + KV_CHUNK_BLOCKS - 1) // KV_CHUNK_BLOCKS

    def block_scores(c, t):
        ks = pl.multiple_of(c * ck + t * MOBA_BLOCK, MOBA_BLOCK)
        kx = jnp.concatenate([k_ref[0, pl.ds(ks, MOBA_BLOCK), :], blk_ref[pl.ds(ks, MOBA_BLOCK), :]], axis=1)
        return _dot(kx, qx)

    def run_step(c, cur, nxt, carry, n_cur=KV_CHUNK_BLOCKS, n_nxt=KV_CHUNK_BLOCKS):
        s_ref, mx_ref = cur
        m = [carry[2 * h] for h in range(HEADS_PER_PAIR)]
        acc = [carry[2 * h + 1] for h in range(HEADS_PER_PAIR)]
        if nxt is None:
            n_nxt = 0
        for t in range(max(n_cur, n_nxt)):
            rows = slice(t * MOBA_BLOCK, (t + 1) * MOBA_BLOCK)
            if t < n_nxt:
                s_nxt = block_scores(c + 1, t)
                nxt[0][rows, :] = s_nxt
                nxt[1][t:t + 1, :] = jnp.max(s_nxt, axis=0, keepdims=True)
            if t >= n_cur:
                continue
            ks = pl.multiple_of(c * ck + t * MOBA_BLOCK, MOBA_BLOCK)
            for h in range(HEADS_PER_PAIR):
                cols = slice(h * tq, (h + 1) * tq)
                m_new = jnp.maximum(m[h], mx_ref[t:t + 1, cols])
                p = jnp.exp(s_ref[rows, cols] - m_new).astype(BF16)
                acc[h] = jnp.exp(m[h] - m_new) * acc[h] + _dot(v_ext(h, ks, MOBA_BLOCK), p)
                m[h] = m_new
        return (m[0], acc[0], m[1], acc[1])

    even, odd = (s_even, mx_even), (s_odd, mx_odd)

    s_own = [jnp.where(key_pos <= qry_pos, _dot(kd, q_heads[h]), MASK_NEG) for h in range(HEADS_PER_PAIR)]
    for t in range(KV_CHUNK_BLOCKS):
        s_first = block_scores(0, t)
        s_even[t * MOBA_BLOCK:(t + 1) * MOBA_BLOCK, :] = s_first
        mx_even[t:t + 1, :] = jnp.max(s_first, axis=0, keepdims=True)
    carry = []
    for h in range(HEADS_PER_PAIR):
        m = jnp.max(s_own[h], axis=0, keepdims=True)
        p = jnp.exp(s_own[h] - m).astype(BF16)
        carry += [m, _dot(v_ext(h, start, MOBA_BLOCK), p)]

    def pair(k, carry):
        carry = run_step(2 * k, even, odd, carry)
        return run_step(2 * k + 1, odd, even, carry)

    carry = lax.fori_loop(0, jnp.maximum(n_chunks - 1, 0) // 2, pair, tuple(carry))

    last_blocks = i - (n_chunks - 1) * KV_CHUNK_BLOCKS

    def last_one(r):
        return lambda: finish(run_step(n_chunks - 1, even, None, carry, n_cur=r))

    def last_two(r):
        def run():
            mid = run_step(n_chunks - 2, even, odd, carry, n_nxt=r)
            return finish(run_step(n_chunks - 1, odd, None, mid, n_cur=r))
        return run

    counts = range(1, KV_CHUNK_BLOCKS + 1)

    def finish(carry):
        heads = []
        for h in range(HEADS_PER_PAIR):
            acc = carry[2 * h + 1]
            heads.append(acc[0:HEAD_DIM] / acc[HEAD_DIM:HEAD_DIM + 1])
        o_ref[0, pl.ds(start, tq), :] = jnp.concatenate(heads, axis=0).T
        return 0

    lax.cond(
        n_chunks % 2 == 1,
        lambda: lax.switch(last_blocks - 1, [last_one(r) for r in counts]),
        lambda: lax.cond(n_chunks > 0,
                         lambda: lax.switch(last_blocks - 1, [last_two(r) for r in counts]),
                         lambda: finish(carry)))


def _moba(q, k, v, bias):
    B, S, _ = k.shape
    nb = S // MOBA_BLOCK
    tq = MOBA_BLOCK
    ck = KV_CHUNK_BLOCKS * MOBA_BLOCK
    block_onehot = (jnp.arange(S)[:, None] // MOBA_BLOCK == jnp.arange(V7X_LANES)[None, :]).astype(BF16)
    feat_major = pl.BlockSpec((1, V7X_LANES, S), lambda b, p: (b, p, 0))
    return pl.pallas_call(
        _moba_kernel,
        grid=(B, N_PAIRS),
        in_specs=[
            feat_major,
            pl.BlockSpec((1, S, V7X_LANES), lambda b, p: (b, 0, p)),
            feat_major,
            pl.BlockSpec((1, HEADS_PER_PAIR * nb, S), lambda b, p: (b, p, 0)),
            pl.BlockSpec((S, V7X_LANES), lambda b, p: (0, 0), pipeline_mode=pl.Buffered(1)),
        ],
        out_specs=pl.BlockSpec((1, S, V7X_LANES), lambda b, p: (b, 0, p)),
        out_shape=jax.ShapeDtypeStruct((B, S, ATTN_WIDTH), F32),
        scratch_shapes=[
            pltpu.VMEM((ck, HEADS_PER_PAIR * tq), F32),
            pltpu.VMEM((ck, HEADS_PER_PAIR * tq), F32),
            pltpu.VMEM((V7X_SUBLANES, HEADS_PER_PAIR * tq), F32),
            pltpu.VMEM((V7X_SUBLANES, HEADS_PER_PAIR * tq), F32),
        ],
        compiler_params=pltpu.CompilerParams(
            dimension_semantics=("arbitrary", "arbitrary"),
            vmem_limit_bytes=V7X_VMEM_LIMIT_BYTES),
        name="moba",
    )(q, k, v, bias, block_onehot)


def _mixer_out_kernel(x_ref, lru_ref, attn_ref, mod_ref, ga_ref, w_out_ref, g2_ref, w_up_ref, cw_ref, cb_ref,
                      w_down_ref, o_ref, up_ext):
    tm = x_ref.shape[1]

    @pl.when(pl.program_id(1) == 0)
    def _():
        up_ext[0:V7X_SUBLANES, :] = jnp.zeros((V7X_SUBLANES, 2 * D_FF), F32)

    mod = mod_ref[0]
    gate1, sh2, sc2, gate2 = mod[2:3, :], mod[3:4, :], mod[4:5, :], mod[5:6, :]

    attn = attn_ref[0]
    attn_n = (attn * _rms_scale(attn) * ga_ref[...]).astype(BF16)
    mixed = _dot(lru_ref[0], w_out_ref[0:LRU_WIDTH, :]) + _dot(attn_n, w_out_ref[LRU_WIDTH:, :])
    x1 = x_ref[0] + gate1 * mixed

    h2 = ((x1 * _rms_scale(x1) * g2_ref[...]) * (1.0 + sc2) + sh2).astype(BF16)

    def conv_cols(c0):
        cols = pl.ds(c0, FF_CHUNK)
        up_ext[V7X_SUBLANES:, cols] = _dot(h2, w_up_ref[:, cols])
        out = cb_ref[:, cols] + cw_ref[FFN_CONV - 1:FFN_CONV, cols] * up_ext[V7X_SUBLANES:, cols]
        for j in range(FFN_CONV - 1):
            back = FFN_CONV - 1 - j
            out = out + cw_ref[j:j + 1, cols] * up_ext[pl.ds(V7X_SUBLANES - back, tm), cols]
        return out

    acc = jnp.zeros((tm, D_MODEL), F32)
    for c in range(D_FF // FF_CHUNK):
        g = conv_cols(c * FF_CHUNK)
        val = conv_cols(D_FF + c * FF_CHUNK)
        act = (g * jax.nn.sigmoid(g) * val).astype(BF16)
        acc = acc + _dot(act, w_down_ref[c * FF_CHUNK:(c + 1) * FF_CHUNK, :])
    up_ext[0:V7X_SUBLANES, :] = up_ext[tm:tm + V7X_SUBLANES, :]

    o_ref[0] = x1 + gate2 * acc


def _mixer_out(x, lru, attn, mod, ga, w_out, g2, w_up, cw, cb, w_down):
    B, S, _ = x.shape
    tm = TM_OUT
    const = lambda shape: pl.BlockSpec(shape, lambda b, s: (0,) * len(shape), pipeline_mode=pl.Buffered(1))
    tok = lambda w: pl.BlockSpec((1, tm, w), lambda b, s: (b, s, 0))
    return pl.pallas_call(
        _mixer_out_kernel,
        grid=(B, S // tm),
        in_specs=[
            tok(D_MODEL),
            tok(LRU_WIDTH),
            tok(ATTN_WIDTH),
            pl.BlockSpec((1, N_MOD, D_MODEL), lambda b, s: (b, 0, 0)),
            const((1, ATTN_WIDTH)),
            const((D_MODEL, D_MODEL)),
            const((1, D_MODEL)),
            const((D_MODEL, 2 * D_FF)),
            const((FFN_CONV, 2 * D_FF)),
            const((1, 2 * D_FF)),
            const((D_FF, D_MODEL)),
        ],
        out_specs=tok(D_MODEL),
        out_shape=jax.ShapeDtypeStruct((B, S, D_MODEL), F32),
        scratch_shapes=[pltpu.VMEM((tm + V7X_SUBLANES, 2 * D_FF), F32)],
        compiler_params=pltpu.CompilerParams(
            dimension_semantics=("arbitrary", "arbitrary"), vmem_limit_bytes=V7X_VMEM_LIMIT_BYTES),
        name="mixer_out",
    )(x, lru, attn, mod, ga, w_out, g2, w_up, cw, cb, w_down)


def _block_diag(w):
    n, d, _ = w.shape
    eye = jnp.eye(n, dtype=w.dtype)
    return jnp.einsum('hij,hg->higj', w, eye).reshape(n * d, n * d)


def _layer(x, c, w_ada, b_ada, norm1_g, w_in, q_norm_g, k_norm_g, lru_conv_w, lru_conv_b, lru_wa, lru_ba,
           lru_wx, lru_bx, lru_lambda, lru_out_g, attn_out_g, w_out, norm2_g, w_up, ffn_conv_w, ffn_conv_b,
           w_down):
    B, S, _ = x.shape
    row = lambda t: t.reshape(1, -1)
    mod = _modulation(c, w_ada, row(b_ada)).reshape(B, N_MOD, D_MODEL)

    head_id = jnp.arange(V7X_MXU_DIM) // HEAD_DIM
    gmat = jnp.where(head_id[:, None] == head_id[None, :], 1.0 / HEAD_DIM, 0.0).astype(BF16)
    heads_per_tile = V7X_MXU_DIM // (LRU_WIDTH // LRU_HEADS)
    w_gates = jnp.stack([
        jnp.concatenate([_block_diag(w[j * heads_per_tile:(j + 1) * heads_per_tile]) for w in (lru_wa, lru_wx)], axis=1)
        for j in range(LRU_HEADS // heads_per_tile)]).astype(BF16)
    b_gates = jnp.concatenate([lru_ba, lru_bx]).reshape(1, -1)

    q, k, v, bias, lru = _mixer_in(
        x, mod, row(norm1_g), w_in.astype(BF16), row(jnp.tile(q_norm_g, ATTN_HEADS)),
        row(jnp.tile(k_norm_g, ATTN_HEADS)), gmat, lru_conv_w, row(lru_conv_b), w_gates, b_gates,
        row(lru_lambda), row(lru_out_g))
    attn = _moba(q, k, v, bias)
    return _mixer_out(x, lru, attn, mod, row(attn_out_g), w_out.astype(BF16), row(norm2_g), w_up.astype(BF16),
                      ffn_conv_w, row(ffn_conv_b), w_down.astype(BF16))


def kernel(x, c, w_ada, b_ada, norm1_g, w_in, q_norm_g, k_norm_g, lru_conv_w, lru_conv_b, lru_wa, lru_ba, lru_wx, lru_bx, lru_lambda, lru_out_g, attn_out_g, w_out, norm2_g, w_up, ffn_conv_w, ffn_conv_b, w_down):
    depth = w_ada.shape[0]
    for l in range(depth):
        x = _layer(x, c, w_ada[l], b_ada[l], norm1_g[l], w_in[l], q_norm_g[l], k_norm_g[l], lru_conv_w[l],
                   lru_conv_b[l], lru_wa[l], lru_ba[l], lru_wx[l], lru_bx[l], lru_lambda[l], lru_out_g[l],
                   attn_out_g[l], w_out[l], norm2_g[l], w_up[l], ffn_conv_w[l], ffn_conv_b[l], w_down[l])
    return x
```

```python
import functools
import math

import jax
import jax.numpy as jnp
from jax import lax
from jax.experimental import pallas as pl
from jax.experimental.pallas import tpu as pltpu

D_MODEL = 1024
ATTN_HEADS = 8
HEAD_DIM = 64
ATTN_WIDTH = ATTN_HEADS * HEAD_DIM
MOBA_BLOCK = 256
MOBA_TOPK = 3
LRU_WIDTH = D_MODEL - ATTN_WIDTH
LRU_HEADS = 8
LRU_CONV = 4
LRU_C = 8.0
N_IN = 3 * ATTN_WIDTH + 2 * LRU_WIDTH
D_FF = 2816
FFN_CONV = 3
N_MOD = 6
EPS = 1e-6

V7X_LANES = 128
V7X_SUBLANES = 8
V7X_MXU_DIM = 256
V7X_VMEM_LIMIT_BYTES = 56 * 1024 * 1024

HEADS_PER_PAIR = V7X_LANES // HEAD_DIM
N_PAIRS = ATTN_HEADS // HEADS_PER_PAIR
MASK_NEG = -1e30

TM_IN = 1024
TM_OUT = 512
KV_CHUNK_BLOCKS = 4
MOBA_TILE_BLOCKS = 1
FF_CHUNKS = (2816,)
assert sum(FF_CHUNKS) == D_FF and all(w % V7X_MXU_DIM == 0 for w in FF_CHUNKS)
MOD_CHUNK = 1536

BF16 = jnp.bfloat16
F32 = jnp.float32


def _dot(a, b):
    return jnp.dot(a, b, preferred_element_type=F32)


def _rms_scale(y):
    return lax.rsqrt(jnp.mean(y * y, axis=-1, keepdims=True) + EPS)


def _mod_kernel(c_ref, w_ref, b_ref, o_ref):
    o_ref[...] = _dot(c_ref[...], w_ref[...]) + b_ref[...]


def _modulation(c, w_ada, b_ada):
    B = c.shape[0]
    n = w_ada.shape[1]
    return pl.pallas_call(
        _mod_kernel,
        grid=(n // MOD_CHUNK,),
        in_specs=[
            pl.BlockSpec((B, D_MODEL), lambda j: (0, 0)),
            pl.BlockSpec((D_MODEL, MOD_CHUNK), lambda j: (0, j)),
            pl.BlockSpec((1, MOD_CHUNK), lambda j: (0, j)),
        ],
        out_specs=pl.BlockSpec((B, MOD_CHUNK), lambda j: (0, j)),
        out_shape=jax.ShapeDtypeStruct((B, n), F32),
        compiler_params=pltpu.CompilerParams(
            dimension_semantics=("arbitrary",), vmem_limit_bytes=V7X_VMEM_LIMIT_BYTES),
        name="modulation",
    )(c, w_ada, b_ada)


def _head_rmsnorm(t, gmat, gain):
    sq = t * t
    hi = sq.astype(BF16)
    lo = (sq - hi.astype(F32)).astype(BF16)
    ms = jnp.concatenate(
        [_dot(hi[:, c:c + V7X_MXU_DIM], gmat) + _dot(lo[:, c:c + V7X_MXU_DIM], gmat)
         for c in range(0, t.shape[1], V7X_MXU_DIM)], axis=1)
    return t * lax.rsqrt(ms + EPS) * gain


def _shift_rows(t, d, fill):
    tm = t.shape[0]
    if d % V7X_SUBLANES == 0:
        pad = jnp.full((d, t.shape[1]), fill, t.dtype)
        return jnp.concatenate([pad, t[: tm - d]], axis=0)
    row = lax.broadcasted_iota(jnp.int32, t.shape, 0)
    return jnp.where(row >= d, pltpu.roll(t, d, 0), fill)


SCAN_RUN = V7X_SUBLANES
SCAN_GROUP = SCAN_RUN * V7X_SUBLANES


def _to_run_major(buf, t):
    tm, width = t.shape
    lane_groups = width // V7X_LANES
    for l in range(lane_groups):
        buf[l] = t[:, l * V7X_LANES:(l + 1) * V7X_LANES]
    pieces = []
    for g in range(tm // SCAN_GROUP):
        for s in range(SCAN_RUN):
            rows = pl.ds(g * SCAN_GROUP + s, V7X_SUBLANES, stride=SCAN_RUN)
            pieces.append(jnp.concatenate([buf[l, rows, :] for l in range(lane_groups)], axis=1))
    return jnp.concatenate(pieces, axis=0)


def _from_run_major(buf, t):
    tm, width = t.shape
    lane_groups = width // V7X_LANES
    for g in range(tm // SCAN_GROUP):
        for s in range(SCAN_RUN):
            piece = t[g * SCAN_GROUP + s * V7X_SUBLANES:g * SCAN_GROUP + (s + 1) * V7X_SUBLANES]
            rows = pl.ds(g * SCAN_GROUP + s, V7X_SUBLANES, stride=SCAN_RUN)
            for l in range(lane_groups):
                buf[l, rows, :] = piece[:, l * V7X_LANES:(l + 1) * V7X_LANES]
    return jnp.concatenate([buf[l] for l in range(lane_groups)], axis=1)


def _causal_conv_run_major(xp, w_ref, b_ref, carry_ref):
    taps = w_ref.shape[0]
    tm, width = xp.shape
    row = lax.broadcasted_iota(jnp.int32, (V7X_SUBLANES, width), 0)
    wrapped = range(SCAN_RUN - (taps - 1), SCAN_RUN)
    above = {s: carry_ref[s:s + 1, :] for s in wrapped}
    bias = b_ref[...]
    out = []
    for g in range(tm // SCAN_GROUP):
        def piece(s):
            return xp[g * SCAN_GROUP + s * V7X_SUBLANES:g * SCAN_GROUP + (s + 1) * V7X_SUBLANES]

        prev_run = {}
        for s in wrapped:
            prev_run[s] = jnp.where(row >= 1, pltpu.roll(piece(s), 1, 0), above[s])
            above[s] = piece(s)[V7X_SUBLANES - 1:V7X_SUBLANES]
        for s in range(SCAN_RUN):
            acc = bias + w_ref[taps - 1:taps, :] * piece(s)
            for d in range(1, taps):
                src = piece(s - d) if s >= d else prev_run[s - d + SCAN_RUN]
                acc = acc + w_ref[taps - 1 - d:taps - d, :] * src
            out.append(acc)
    for s in wrapped:
        carry_ref[s:s + 1, :] = above[s]
    return jnp.concatenate(out, axis=0)


def _linear_scan_run_major(a, u, h_in):
    tm = a.shape[0]
    row = lax.broadcasted_iota(jnp.int32, (V7X_SUBLANES, a.shape[1]), 0)
    out = []
    for g in range(tm // SCAN_GROUP):
        h_loc, a_run = [], []
        for s in range(SCAN_RUN):
            rows = slice(g * SCAN_GROUP + s * V7X_SUBLANES, g * SCAN_GROUP + (s + 1) * V7X_SUBLANES)
            h_loc.append(u[rows] if s == 0 else a[rows] * h_loc[-1] + u[rows])
            a_run.append(a[rows] if s == 0 else a[rows] * a_run[-1])
        h_end, a_end = h_loc[-1], a_run[-1]
        d = 1
        while d < V7X_SUBLANES:
            h_end = a_end * _shift_rows(h_end, d, 0.0) + h_end
            a_end = a_end * _shift_rows(a_end, d, 1.0)
            d *= 2
        after = h_end + a_end * h_in
        before = jnp.where(row >= 1, pltpu.roll(after, 1, 0), h_in)
        h_in = after[V7X_SUBLANES - 1:V7X_SUBLANES]
        out += [h_loc[s] + a_run[s] * before for s in range(SCAN_RUN)]
    return jnp.concatenate(out, axis=0), h_in


def _gelu_tanh(t):
    return 0.5 * t * (1.0 + jnp.tanh(math.sqrt(2.0 / math.pi) * (t + 0.044715 * (t * t * t))))


def _mixer_in_kernel(x_ref, mod_ref, g1_ref, w_in_ref, gq_ref, gk_ref, gmat_ref, cw_ref, cb_ref,
                     wg_ref, bg_ref, lam_ref, go_ref,
                     q_ref, k_ref, v_ref, bias_ref, lru_ref,
                     conv_carry, h_carry, km_all, scan_buf):
    @pl.when(pl.program_id(1) == 0)
    def _():
        conv_carry[...] = jnp.zeros_like(conv_carry)
        h_carry[...] = jnp.zeros_like(h_carry)
        km_all[...] = jnp.zeros_like(km_all)

    blocks_per_step = x_ref.shape[1] // MOBA_BLOCK
    mod = mod_ref[0]
    gain = g1_ref[...] * (1.0 + mod[1:2, :])
    shift = mod[0:1, :]
    normed = []
    for sub in range(blocks_per_step):
        x = x_ref[0, sub * MOBA_BLOCK:(sub + 1) * MOBA_BLOCK, :]
        normed.append(((x * _rms_scale(x)) * gain + shift).astype(BF16))
    for sub in range(blocks_per_step):
        _attention_side(sub, pl.program_id(1) * blocks_per_step + sub, normed[sub], w_in_ref, gq_ref, gk_ref,
                        gmat_ref, q_ref, k_ref, v_ref, bias_ref, km_all)
    for sub in range(blocks_per_step):
        _lru_side(sub, normed[sub], w_in_ref, cw_ref, cb_ref, wg_ref, bg_ref, lam_ref, go_ref, lru_ref,
                  conv_carry, h_carry, scan_buf)


def _in_proj(hb, w_in_ref, j):
    return _dot(hb, w_in_ref[:, j * ATTN_WIDTH:(j + 1) * ATTN_WIDTH])


def _attention_side(sub, blk_id, hb, w_in_ref, gq_ref, gk_ref, gmat_ref, q_ref, k_ref, v_ref, bias_ref, km_all):
    tm = MOBA_BLOCK
    rows = slice(sub * tm, (sub + 1) * tm)
    proj = functools.partial(_in_proj, hb, w_in_ref)
    gmat = gmat_ref[...]

    kn = _head_rmsnorm(proj(1), gmat, gk_ref[...])
    k_ref[0, rows, :] = kn.astype(BF16)
    km_all[pl.ds(blk_id, 1), :] = jnp.mean(kn, axis=0, keepdims=True)

    qn =_head_rmsnorm(proj(0), gmat, gq_ref[...] * (HEAD_DIM ** -0.5))
    q_t = qn.T.astype(BF16)
    q_ref[0, :, rows] = q_t

    nb = km_all.shape[0]
    km = km_all[...].astype(BF16)
    feat = lax.broadcasted_iota(jnp.int32, (V7X_LANES, 1), 0)
    for head in range(ATTN_HEADS):
        p, h = divmod(head, HEADS_PER_PAIR)
        q_pair = q_t[p * V7X_LANES:(p + 1) * V7X_LANES]
        qh = jnp.where((feat >= h * HEAD_DIM) & (feat < (h + 1) * HEAD_DIM), q_pair, jnp.zeros_like(q_pair))
        gate = _dot(km[:, p * V7X_LANES:(p + 1) * V7X_LANES], qh)
        bias_ref[0, head * nb:(head + 1) * nb, rows] = _select_bias(gate, blk_id).astype(BF16)

    v_ref[0, :, rows] = proj(2).T.astype(BF16)


def _lru_side(sub, hb, w_in_ref, cw_ref, cb_ref, wg_ref, bg_ref, lam_ref, go_ref, lru_ref, conv_carry, h_carry,
              scan_buf):
    tm = MOBA_BLOCK
    rows = slice(sub * tm, (sub + 1) * tm)
    proj = functools.partial(_in_proj, hb, w_in_ref)

    xc = _causal_conv_run_major(_to_run_major(scan_buf.at[2 * sub], proj(3)), cw_ref, cb_ref, conv_carry)
    xcb = xc.astype(BF16)
    pre = [_dot(xcb[:, j * V7X_MXU_DIM:(j + 1) * V7X_MXU_DIM], wg_ref[j]) for j in range(LRU_WIDTH // V7X_MXU_DIM)]
    bg = bg_ref[...]
    r = jax.nn.sigmoid(jnp.concatenate([g[:, :V7X_MXU_DIM] for g in pre], axis=1) + bg[:, :LRU_WIDTH])
    i = jax.nn.sigmoid(jnp.concatenate([g[:, V7X_MXU_DIM:] for g in pre], axis=1) + bg[:, LRU_WIDTH:])
    neg_lam = -lam_ref[...]
    softplus = jnp.maximum(neg_lam, 0.0) + jnp.log1p(jnp.exp(-jnp.abs(neg_lam)))
    log_a = r * ((-LRU_C) * softplus)
    a = jnp.exp(log_a)
    th = jnp.tanh(log_a)
    u = jnp.sqrt(-2.0 * th / (1.0 - th)) * (i * xc)
    hs, h_last = _linear_scan_run_major(a, u, h_carry[0:1, :])
    h_carry[0:1, :] = h_last
    hs = _from_run_major(scan_buf.at[2 * sub + 1], hs)

    y = hs * _gelu_tanh(proj(4))
    lru_ref[0, rows, :] = (y * _rms_scale(y) * go_ref[...]).astype(BF16)


def _mixer_in(x, mod, g1, w_in, gq, gk, gmat, cw, cb, wg, bg, lam, go):
    B, S, _ = x.shape
    tm = TM_IN
    assert tm % MOBA_BLOCK == 0, "a grid step covers whole MoBA blocks: the block choice reads the means of earlier blocks"
    nb = S // MOBA_BLOCK
    const = lambda shape: pl.BlockSpec(shape, lambda b, s: (0,) * len(shape))
    tok = lambda w: pl.BlockSpec((1, tm, w), lambda b, s: (b, s, 0))
    feat_major = pl.BlockSpec((1, ATTN_WIDTH, tm), lambda b, s: (b, 0, s))
    return pl.pallas_call(
        _mixer_in_kernel,
        grid=(B, S // tm),
        in_specs=[
            tok(D_MODEL),
            pl.BlockSpec((1, N_MOD, D_MODEL), lambda b, s: (b, 0, 0)),
            const((1, D_MODEL)),
            const((D_MODEL, N_IN)),
            const((1, ATTN_WIDTH)),
            const((1, ATTN_WIDTH)),
            const((V7X_MXU_DIM, V7X_MXU_DIM)),
            const((LRU_CONV, LRU_WIDTH)),
            const((1, LRU_WIDTH)),
            const((LRU_WIDTH // V7X_MXU_DIM, V7X_MXU_DIM, 2 * V7X_MXU_DIM)),
            const((1, 2 * LRU_WIDTH)),
            const((1, LRU_WIDTH)),
            const((1, LRU_WIDTH)),
        ],
        out_specs=[
            feat_major,
            tok(ATTN_WIDTH),
            feat_major,
            pl.BlockSpec((1, ATTN_HEADS * nb, tm), lambda b, s: (b, 0, s)),
            tok(LRU_WIDTH),
        ],
        out_shape=[
            jax.ShapeDtypeStruct((B, ATTN_WIDTH, S), BF16),
            jax.ShapeDtypeStruct((B, S, ATTN_WIDTH), BF16),
            jax.ShapeDtypeStruct((B, ATTN_WIDTH, S), BF16),
            jax.ShapeDtypeStruct((B, ATTN_HEADS * nb, S), BF16),
            jax.ShapeDtypeStruct((B, S, LRU_WIDTH), BF16),
        ],
        scratch_shapes=[
            pltpu.VMEM((V7X_SUBLANES, LRU_WIDTH), F32),
            pltpu.VMEM((V7X_SUBLANES, LRU_WIDTH), F32),
            pltpu.VMEM((nb, ATTN_WIDTH), F32),
            pltpu.VMEM((2 * (tm // MOBA_BLOCK), LRU_WIDTH // V7X_LANES, MOBA_BLOCK, V7X_LANES), F32),
        ],
        compiler_params=pltpu.CompilerParams(
            dimension_semantics=("arbitrary", "arbitrary"), vmem_limit_bytes=V7X_VMEM_LIMIT_BYTES),
        name="mixer_in",
    )(x, mod, g1, w_in, gq, gk, gmat, cw, cb, wg, bg, lam, go)


def _select_bias(gate, n_past):
    nb = gate.shape[0]
    blk = lax.broadcasted_iota(jnp.int32, gate.shape, 0)
    neg_inf = jnp.float32(-jnp.inf)
    g = jnp.where(blk < n_past, gate, neg_inf)
    bias = jnp.full(gate.shape, MASK_NEG, F32)
    for _ in range(MOBA_TOPK):
        mx = jnp.max(g, axis=0, keepdims=True)
        cand = jnp.where((g == mx) & (mx > neg_inf), blk, nb)
        pick = blk == jnp.min(cand, axis=0, keepdims=True)
        bias = jnp.where(pick, 0.0, bias)
        g = jnp.where(pick, neg_inf, g)
    return bias


def _moba_kernel(q_ref, k_ref, v_ref, bias_ref, blk_ref, o_ref, s_even, s_odd, mx_even, mx_odd):
    def tile(i, _):
        _moba_tile(i, q_ref, k_ref, v_ref, bias_ref, blk_ref, o_ref, s_even, s_odd, mx_even, mx_odd)
        return 0

    lax.fori_loop(0, q_ref.shape[2] // (MOBA_TILE_BLOCKS * MOBA_BLOCK), tile, 0)


def _moba_tile(it, q_ref, k_ref, v_ref, bias_ref, blk_ref, o_ref, s_even, s_odd, mx_even, mx_odd):
    tq = MOBA_TILE_BLOCKS * MOBA_BLOCK
    nb = bias_ref.shape[1] // HEADS_PER_PAIR
    feat = lax.broadcasted_iota(jnp.int32, (V7X_LANES, 1), 0)
    start = pl.multiple_of(it * tq, tq)
    i = it * MOBA_TILE_BLOCKS + (MOBA_TILE_BLOCKS - 1)
    q_t = q_ref[0, :, pl.ds(start, tq)]

    def v_ext(h, start, n):
        vh = v_ref[0, h * HEAD_DIM:(h + 1) * HEAD_DIM, pl.ds(start, n)]
        return jnp.concatenate([vh, jnp.ones((2 * V7X_SUBLANES, n), BF16)], axis=0)

    key_pos = lax.broadcasted_iota(jnp.int32, (MOBA_BLOCK, MOBA_BLOCK), 0)
    qry_pos = lax.broadcasted_iota(jnp.int32, (MOBA_BLOCK, MOBA_BLOCK), 1)
    feat_pad = jnp.zeros((V7X_LANES - nb, tq), BF16)

    q_heads = [jnp.where((feat >= h * HEAD_DIM) & (feat < (h + 1) * HEAD_DIM), q_t, jnp.zeros_like(q_t))
               for h in range(HEADS_PER_PAIR)]
    qx = jnp.concatenate(
        [jnp.concatenate([q_heads[h], bias_ref[0, h * nb:(h + 1) * nb, pl.ds(start, tq)], feat_pad], axis=0)
         for h in range(HEADS_PER_PAIR)], axis=1)

    ck = KV_CHUNK_BLOCKS * MOBA_BLOCK
    n_chunks = (i + KV_CHUNK_BLOCKS - 1) // KV_CHUNK_BLOCKS

    def block_scores(c, t):
        ks = pl.multiple_of(c * ck + t * MOBA_BLOCK, MOBA_BLOCK)
        kx = jnp.concatenate([k_ref[0, pl.ds(ks, MOBA_BLOCK), :], blk_ref[pl.ds(ks, MOBA_BLOCK), :]], axis=1)
        return _dot(kx, qx)

    def run_step(c, cur, nxt, carry, n_cur=KV_CHUNK_BLOCKS, n_nxt=KV_CHUNK_BLOCKS):
        s_ref, mx_ref = cur
        m = [carry[2 * h] for h in range(HEADS_PER_PAIR)]
        acc = [carry[2 * h + 1] for h in range(HEADS_PER_PAIR)]
        if nxt is None:
            n_nxt = 0
        for t in range(max(n_cur, n_nxt)):
            rows = slice(t * MOBA_BLOCK, (t + 1) * MOBA_BLOCK)
            if t < n_nxt:
                s_nxt = block_scores(c + 1, t)
                nxt[0][rows, :] = s_nxt
                nxt[1][t:t + 1, :] = jnp.max(s_nxt, axis=0, keepdims=True)
            if t >= n_cur:
                continue
            ks = pl.multiple_of(c * ck + t * MOBA_BLOCK, MOBA_BLOCK)
            for h in range(HEADS_PER_PAIR):
                cols = slice(h * tq, (h + 1) * tq)
                m_new = jnp.maximum(m[h], mx_ref[t:t + 1, cols])
                p = jnp.exp(s_ref[rows, cols] - m_new).astype(BF16)
                acc[h] = jnp.exp(m[h] - m_new) * acc[h] + _dot(v_ext(h, ks, MOBA_BLOCK), p)
                m[h] = m_new
        return (m[0], acc[0], m[1], acc[1])

    even, odd = (s_even, mx_even), (s_odd, mx_odd)

    own = [(h, u) for h in range(HEADS_PER_PAIR) for u in range(MOBA_TILE_BLOCKS)]
    s_own = {}
    own_start = [pl.multiple_of(start + u * MOBA_BLOCK, MOBA_BLOCK) for u in range(MOBA_TILE_BLOCKS)]
    for h, u in own:
        kd = k_ref[0, pl.ds(own_start[u], MOBA_BLOCK), :]
        s = _dot(kd, q_heads[h][:, u * MOBA_BLOCK:(u + 1) * MOBA_BLOCK])
        s_own[h, u] = jnp.where(key_pos <= qry_pos, s, MASK_NEG)
    for t in range(KV_CHUNK_BLOCKS):
        s_first = block_scores(0, t)
        s_even[t * MOBA_BLOCK:(t + 1) * MOBA_BLOCK, :] = s_first
        mx_even[t:t + 1, :] = jnp.max(s_first, axis=0, keepdims=True)
    carry = []
    for h in range(HEADS_PER_PAIR):
        m_own, acc_own = [], []
        for u in range(MOBA_TILE_BLOCKS):
            m = jnp.max(s_own[h, u], axis=0, keepdims=True)
            p = jnp.exp(s_own[h, u] - m).astype(BF16)
            m_own.append(m)
            acc_own.append(_dot(v_ext(h, own_start[u], MOBA_BLOCK), p))
        carry += [jnp.concatenate(m_own, axis=1), jnp.concatenate(acc_own, axis=1)]

    def pair(k, carry):
        carry = run_step(2 * k, even, odd, carry)
        return run_step(2 * k + 1, odd, even, carry)

    carry = lax.fori_loop(0, jnp.maximum(n_chunks - 1, 0) // 2, pair, tuple(carry))

    last_blocks = i - (n_chunks - 1) * KV_CHUNK_BLOCKS

    def last_one(r):
        return lambda: finish(run_step(n_chunks - 1, even, None, carry, n_cur=r))

    def last_two(r):
        def run():
            mid = run_step(n_chunks - 2, even, odd, carry, n_nxt=r)
            return finish(run_step(n_chunks - 1, odd, None, mid, n_cur=r))
        return run

    counts = range(1, KV_CHUNK_BLOCKS + 1)

    def finish(carry):
        heads = []
        for h in range(HEADS_PER_PAIR):
            acc = carry[2 * h + 1]
            heads.append(acc[0:HEAD_DIM] / acc[HEAD_DIM:HEAD_DIM + 1])
        o_ref[0, pl.ds(start, tq), :] = jnp.concatenate(heads, axis=0).T
        return 0

    lax.cond(
        n_chunks % 2 == 1,
        lambda: lax.switch(last_blocks - 1, [last_one(r) for r in counts]),
        lambda: lax.cond(n_chunks > 0,
                         lambda: lax.switch(last_blocks - 1, [last_two(r) for r in counts]),
                         lambda: finish(carry)))


def _moba(q, k, v, bias):
    B, S, _ = k.shape
    nb = S // MOBA_BLOCK
    tq = MOBA_TILE_BLOCKS * MOBA_BLOCK
    ck = KV_CHUNK_BLOCKS * MOBA_BLOCK
    block_onehot = (jnp.arange(S)[:, None] // MOBA_BLOCK == jnp.arange(V7X_LANES)[None, :]).astype(BF16)
    feat_major = pl.BlockSpec((1, V7X_LANES, S), lambda b, p: (b, p, 0))
    return pl.pallas_call(
        _moba_kernel,
        grid=(B, N_PAIRS),
        in_specs=[
            feat_major,
            pl.BlockSpec((1, S, V7X_LANES), lambda b, p: (b, 0, p)),
            feat_major,
            pl.BlockSpec((1, HEADS_PER_PAIR * nb, S), lambda b, p: (b, p, 0)),
            pl.BlockSpec((S, V7X_LANES), lambda b, p: (0, 0), pipeline_mode=pl.Buffered(1)),
        ],
        out_specs=pl.BlockSpec((1, S, V7X_LANES), lambda b, p: (b, 0, p)),
        out_shape=jax.ShapeDtypeStruct((B, S, ATTN_WIDTH), F32),
        scratch_shapes=[
            pltpu.VMEM((ck, HEADS_PER_PAIR * tq), F32),
            pltpu.VMEM((ck, HEADS_PER_PAIR * tq), F32),
            pltpu.VMEM((V7X_SUBLANES, HEADS_PER_PAIR * tq), F32),
            pltpu.VMEM((V7X_SUBLANES, HEADS_PER_PAIR * tq), F32),
        ],
        compiler_params=pltpu.CompilerParams(
            dimension_semantics=("arbitrary", "arbitrary"),
            vmem_limit_bytes=V7X_VMEM_LIMIT_BYTES),
        name="moba",
    )(q, k, v, bias, block_onehot)


def _mixer_out_kernel(x_ref, lru_ref, attn_ref, mod_ref, ga_ref, w_out_ref, g2_ref, w_up_ref, cw_ref, cb_ref,
                      w_down_ref, o_ref, up_ext):
    tm = x_ref.shape[1]

    @pl.when(pl.program_id(1) == 0)
    def _():
        up_ext[0:V7X_SUBLANES, :] = jnp.zeros((V7X_SUBLANES, 2 * D_FF), F32)

    mod = mod_ref[0]
    gate1, sh2, sc2, gate2 = mod[2:3, :], mod[3:4, :], mod[4:5, :], mod[5:6, :]

    attn = attn_ref[0]
    attn_n = (attn * _rms_scale(attn) * ga_ref[...]).astype(BF16)
    mixed = _dot(lru_ref[0], w_out_ref[0:LRU_WIDTH, :]) + _dot(attn_n, w_out_ref[LRU_WIDTH:, :])
    x1 = x_ref[0] + gate1 * mixed

    h2 = ((x1 * _rms_scale(x1) * g2_ref[...]) * (1.0 + sc2) + sh2).astype(BF16)

    def conv_cols(c0, width):
        cols = pl.ds(c0, width)
        up_ext[V7X_SUBLANES:, cols] = _dot(h2, w_up_ref[:, cols])
        out = cb_ref[:, cols] + cw_ref[FFN_CONV - 1:FFN_CONV, cols] * up_ext[V7X_SUBLANES:, cols]
        for j in range(FFN_CONV - 1):
            back = FFN_CONV - 1 - j
            out = out + cw_ref[j:j + 1, cols] * up_ext[pl.ds(V7X_SUBLANES - back, tm), cols]
        return out

    acc = jnp.zeros((tm, D_MODEL), F32)
    c0 = 0
    for width in FF_CHUNKS:
        g = conv_cols(c0, width)
        val = conv_cols(D_FF + c0, width)
        act = (g * jax.nn.sigmoid(g) * val).astype(BF16)
        acc = acc + _dot(act, w_down_ref[c0:c0 + width, :])
        c0 += width
    up_ext[0:V7X_SUBLANES, :] = up_ext[tm:tm + V7X_SUBLANES, :]

    o_ref[0] = x1 + gate2 * acc


def _mixer_out(x, lru, attn, mod, ga, w_out, g2, w_up, cw, cb, w_down):
    B, S, _ = x.shape
    tm = TM_OUT
    const = lambda shape: pl.BlockSpec(shape, lambda b, s: (0,) * len(shape), pipeline_mode=pl.Buffered(1))
    tok = lambda w: pl.BlockSpec((1, tm, w), lambda b, s: (b, s, 0))
    return pl.pallas_call(
        _mixer_out_kernel,
        grid=(B, S // tm),
        in_specs=[
            tok(D_MODEL),
            tok(LRU_WIDTH),
            tok(ATTN_WIDTH),
            pl.BlockSpec((1, N_MOD, D_MODEL), lambda b, s: (b, 0, 0)),
            const((1, ATTN_WIDTH)),
            const((D_MODEL, D_MODEL)),
            const((1, D_MODEL)),
            const((D_MODEL, 2 * D_FF)),
            const((FFN_CONV, 2 * D_FF)),
            const((1, 2 * D_FF)),
            const((D_FF, D_MODEL)),
        ],
        out_specs=tok(D_MODEL),
        out_shape=jax.ShapeDtypeStruct((B, S, D_MODEL), F32),
        scratch_shapes=[pltpu.VMEM((tm + V7X_SUBLANES, 2 * D_FF), F32)],
        compiler_params=pltpu.CompilerParams(
            dimension_semantics=("arbitrary", "arbitrary"), vmem_limit_bytes=V7X_VMEM_LIMIT_BYTES),
        name="mixer_out",
    )(x, lru, attn, mod, ga, w_out, g2, w_up, cw, cb, w_down)


def _block_diag(w):
    n, d, _ = w.shape
    eye = jnp.eye(n, dtype=w.dtype)
    return jnp.einsum('hij,hg->higj', w, eye).reshape(n * d, n * d)


def _layer(x, c, w_ada, b_ada, norm1_g, w_in, q_norm_g, k_norm_g, lru_conv_w, lru_conv_b, lru_wa, lru_ba,
           lru_wx, lru_bx, lru_lambda, lru_out_g, attn_out_g, w_out, norm2_g, w_up, ffn_conv_w, ffn_conv_b,
           w_down):
    B, S, _ = x.shape
    row = lambda t: t.reshape(1, -1)
    mod = _modulation(c, w_ada, row(b_ada)).reshape(B, N_MOD, D_MODEL)

    head_id = jnp.arange(V7X_MXU_DIM) // HEAD_DIM
    gmat = jnp.where(head_id[:, None] == head_id[None, :], 1.0 / HEAD_DIM, 0.0).astype(BF16)
    heads_per_tile = V7X_MXU_DIM // (LRU_WIDTH // LRU_HEADS)
    w_gates = jnp.stack([
        jnp.concatenate([_block_diag(w[j * heads_per_tile:(j + 1) * heads_per_tile]) for w in (lru_wa, lru_wx)], axis=1)
        for j in range(LRU_HEADS // heads_per_tile)]).astype(BF16)
    b_gates = jnp.concatenate([lru_ba, lru_bx]).reshape(1, -1)

    q, k, v, bias, lru = _mixer_in(
        x, mod, row(norm1_g), w_in.astype(BF16), row(jnp.tile(q_norm_g, ATTN_HEADS)),
        row(jnp.tile(k_norm_g, ATTN_HEADS)), gmat, lru_conv_w, row(lru_conv_b), w_gates, b_gates,
        row(lru_lambda), row(lru_out_g))
    attn = _moba(q, k, v, bias)
    return _mixer_out(x, lru, attn, mod, row(attn_out_g), w_out.astype(BF16), row(norm2_g), w_up.astype(BF16),
                      ffn_conv_w, row(ffn_conv_b), w_down.astype(BF16))


def kernel(x, c, w_ada, b_ada, norm1_g, w_in, q_norm_g, k_norm_g, lru_conv_w, lru_conv_b, lru_wa, lru_ba, lru_wx, lru_bx, lru_lambda, lru_out_g, attn_out_g, w_out, norm2_g, w_up, ffn_conv_w, ffn_conv_b, w_down):
    depth = w_ada.shape[0]
    for l in range(depth):
        x = _layer(x, c, w_ada[l], b_ada[l], norm1_g[l], w_in[l], q_norm_g[l], k_norm_g[l], lru_conv_w[l],
                   lru_conv_b[l], lru_wa[l], lru_ba[l], lru_wx[l], lru_bx[l], lru_lambda[l], lru_out_g[l],
                   attn_out_g[l], w_out[l], norm2_g[l], w_up[l], ffn_conv_w[l], ffn_conv_b[l], w_down[l])
    return x
```

```python
import functools
import math

import jax
import jax.numpy as jnp
from jax import lax
from jax.experimental import pallas as pl
from jax.experimental.pallas import tpu as pltpu

D_MODEL = 1024
ATTN_HEADS = 8
HEAD_DIM = 64
ATTN_WIDTH = ATTN_HEADS * HEAD_DIM
MOBA_BLOCK = 256
MOBA_TOPK = 3
LRU_WIDTH = D_MODEL - ATTN_WIDTH
LRU_HEADS = 8
LRU_CONV = 4
LRU_C = 8.0
N_IN = 3 * ATTN_WIDTH + 2 * LRU_WIDTH
D_FF = 2816
FFN_CONV = 3
N_MOD = 6
EPS = 1e-6

V7X_LANES = 128
V7X_SUBLANES = 8
V7X_MXU_DIM = 256
V7X_VMEM_LIMIT_BYTES = 56 * 1024 * 1024

HEADS_PER_PAIR = V7X_LANES // HEAD_DIM
N_PAIRS = ATTN_HEADS // HEADS_PER_PAIR
MASK_NEG = -1e30

TM_IN = 1024
TM_OUT = 512
KV_CHUNK_BLOCKS = 4
MOBA_TILE_BLOCKS = 1
FF_CHUNKS = (2816,)
assert sum(FF_CHUNKS) == D_FF and all(w % V7X_MXU_DIM == 0 for w in FF_CHUNKS)
MOD_CHUNK = 1536

BF16 = jnp.bfloat16
F32 = jnp.float32


def _dot(a, b):
    return jnp.dot(a, b, preferred_element_type=F32)


def _rms_scale(y):
    return lax.rsqrt(jnp.mean(y * y, axis=-1, keepdims=True) + EPS)


def _mod_kernel(c_ref, w_ref, b_ref, o_ref):
    o_ref[...] = _dot(c_ref[...], w_ref[...]) + b_ref[...]


def _modulation(c, w_ada, b_ada):
    B = c.shape[0]
    n = w_ada.shape[1]
    return pl.pallas_call(
        _mod_kernel,
        grid=(n // MOD_CHUNK,),
        in_specs=[
            pl.BlockSpec((B, D_MODEL), lambda j: (0, 0)),
            pl.BlockSpec((D_MODEL, MOD_CHUNK), lambda j: (0, j)),
            pl.BlockSpec((1, MOD_CHUNK), lambda j: (0, j)),
        ],
        out_specs=pl.BlockSpec((B, MOD_CHUNK), lambda j: (0, j)),
        out_shape=jax.ShapeDtypeStruct((B, n), F32),
        compiler_params=pltpu.CompilerParams(
            dimension_semantics=("arbitrary",), vmem_limit_bytes=V7X_VMEM_LIMIT_BYTES),
        name="modulation",
    )(c, w_ada, b_ada)


def _head_rmsnorm(t, gmat, gain):
    sq = t * t
    hi = sq.astype(BF16)
    lo = (sq - hi.astype(F32)).astype(BF16)
    ms = jnp.concatenate(
        [_dot(hi[:, c:c + V7X_MXU_DIM], gmat) + _dot(lo[:, c:c + V7X_MXU_DIM], gmat)
         for c in range(0, t.shape[1], V7X_MXU_DIM)], axis=1)
    return t * lax.rsqrt(ms + EPS) * gain


def _shift_rows(t, d, fill):
    tm = t.shape[0]
    if d % V7X_SUBLANES == 0:
        pad = jnp.full((d, t.shape[1]), fill, t.dtype)
        return jnp.concatenate([pad, t[: tm - d]], axis=0)
    row = lax.broadcasted_iota(jnp.int32, t.shape, 0)
    return jnp.where(row >= d, pltpu.roll(t, d, 0), fill)


SCAN_RUN = V7X_SUBLANES
SCAN_GROUP = SCAN_RUN * V7X_SUBLANES


def _to_run_major(buf, t):
    tm, width = t.shape
    lane_groups = width // V7X_LANES
    for l in range(lane_groups):
        buf[l] = t[:, l * V7X_LANES:(l + 1) * V7X_LANES]
    pieces = []
    for g in range(tm // SCAN_GROUP):
        for s in range(SCAN_RUN):
            rows = pl.ds(g * SCAN_GROUP + s, V7X_SUBLANES, stride=SCAN_RUN)
            pieces.append(jnp.concatenate([buf[l, rows, :] for l in range(lane_groups)], axis=1))
    return jnp.concatenate(pieces, axis=0)


def _from_run_major(buf, t):
    tm, width = t.shape
    lane_groups = width // V7X_LANES
    for g in range(tm // SCAN_GROUP):
        for s in range(SCAN_RUN):
            piece = t[g * SCAN_GROUP + s * V7X_SUBLANES:g * SCAN_GROUP + (s + 1) * V7X_SUBLANES]
            rows = pl.ds(g * SCAN_GROUP + s, V7X_SUBLANES, stride=SCAN_RUN)
            for l in range(lane_groups):
                buf[l, rows, :] = piece[:, l * V7X_LANES:(l + 1) * V7X_LANES]
    return jnp.concatenate([buf[l] for l in range(lane_groups)], axis=1)


def _causal_conv_run_major(xp, w_ref, b_ref, carry_ref, cols=slice(None)):
    taps = w_ref.shape[0]
    tm, width = xp.shape
    row = lax.broadcasted_iota(jnp.int32, (V7X_SUBLANES, width), 0)
    wrapped = range(SCAN_RUN - (taps - 1), SCAN_RUN)
    above = {s: carry_ref[s:s + 1, cols] for s in wrapped}
    bias = b_ref[:, cols]
    w = [w_ref[k:k + 1, cols] for k in range(taps)]
    out = []
    for g in range(tm // SCAN_GROUP):
        def piece(s):
            return xp[g * SCAN_GROUP + s * V7X_SUBLANES:g * SCAN_GROUP + (s + 1) * V7X_SUBLANES]

        prev_run = {}
        for s in wrapped:
            prev_run[s] = jnp.where(row >= 1, pltpu.roll(piece(s), 1, 0), above[s])
            above[s] = piece(s)[V7X_SUBLANES - 1:V7X_SUBLANES]
        for s in range(SCAN_RUN):
            acc = bias + w[taps - 1] * piece(s)
            for d in range(1, taps):
                src = piece(s - d) if s >= d else prev_run[s - d + SCAN_RUN]
                acc = acc + w[taps - 1 - d] * src
            out.append(acc)
    for s in wrapped:
        carry_ref[s:s + 1, cols] = above[s]
    return jnp.concatenate(out, axis=0)


def _linear_scan_run_major(a, u, h_in):
    tm = a.shape[0]
    row = lax.broadcasted_iota(jnp.int32, (V7X_SUBLANES, a.shape[1]), 0)
    out = []
    for g in range(tm // SCAN_GROUP):
        h_loc, a_run = [], []
        for s in range(SCAN_RUN):
            rows = slice(g * SCAN_GROUP + s * V7X_SUBLANES, g * SCAN_GROUP + (s + 1) * V7X_SUBLANES)
            h_loc.append(u[rows] if s == 0 else a[rows] * h_loc[-1] + u[rows])
            a_run.append(a[rows] if s == 0 else a[rows] * a_run[-1])
        h_end, a_end = h_loc[-1], a_run[-1]
        d = 1
        while d < V7X_SUBLANES:
            h_end = a_end * _shift_rows(h_end, d, 0.0) + h_end
            a_end = a_end * _shift_rows(a_end, d, 1.0)
            d *= 2
        after = h_end + a_end * h_in
        before = jnp.where(row >= 1, pltpu.roll(after, 1, 0), h_in)
        h_in = after[V7X_SUBLANES - 1:V7X_SUBLANES]
        out += [h_loc[s] + a_run[s] * before for s in range(SCAN_RUN)]
    return jnp.concatenate(out, axis=0), h_in


def _gelu_tanh(t):
    return 0.5 * t * (1.0 + jnp.tanh(math.sqrt(2.0 / math.pi) * (t + 0.044715 * (t * t * t))))


def _mixer_in_kernel(x_ref, mod_ref, g1_ref, w_in_ref, gq_ref, gk_ref, gmat_ref, cw_ref, cb_ref,
                     wg_ref, bg_ref, lam_ref, go_ref,
                     q_ref, k_ref, v_ref, bias_ref, lru_ref,
                     conv_carry, h_carry, km_all, scan_buf):
    @pl.when(pl.program_id(1) == 0)
    def _():
        conv_carry[...] = jnp.zeros_like(conv_carry)
        h_carry[...] = jnp.zeros_like(h_carry)
        km_all[...] = jnp.zeros_like(km_all)

    blocks_per_step = x_ref.shape[1] // MOBA_BLOCK
    mod = mod_ref[0]
    gain = g1_ref[...] * (1.0 + mod[1:2, :])
    shift = mod[0:1, :]
    normed = []
    for sub in range(blocks_per_step):
        x = x_ref[0, sub * MOBA_BLOCK:(sub + 1) * MOBA_BLOCK, :]
        normed.append(((x * _rms_scale(x)) * gain + shift).astype(BF16))
    for sub in range(blocks_per_step):
        _attention_side(sub, pl.program_id(1) * blocks_per_step + sub, normed[sub], w_in_ref, gq_ref, gk_ref,
                        gmat_ref, q_ref, k_ref, v_ref, bias_ref, km_all)
    for sub in range(blocks_per_step):
        _lru_side(sub, normed[sub], w_in_ref, cw_ref, cb_ref, wg_ref, bg_ref, lam_ref, go_ref, lru_ref,
                  conv_carry, h_carry, scan_buf)


def _in_proj(hb, w_in_ref, j):
    return _dot(hb, w_in_ref[:, j * ATTN_WIDTH:(j + 1) * ATTN_WIDTH])


def _attention_side(sub, blk_id, hb, w_in_ref, gq_ref, gk_ref, gmat_ref, q_ref, k_ref, v_ref, bias_ref, km_all):
    tm = MOBA_BLOCK
    rows = slice(sub * tm, (sub + 1) * tm)
    proj = functools.partial(_in_proj, hb, w_in_ref)
    gmat = gmat_ref[...]

    kn = _head_rmsnorm(proj(1), gmat, gk_ref[...])
    k_ref[0, rows, :] = kn.astype(BF16)
    km_all[pl.ds(blk_id, 1), :] = jnp.mean(kn, axis=0, keepdims=True)

    qn =_head_rmsnorm(proj(0), gmat, gq_ref[...] * (HEAD_DIM ** -0.5))
    q_t = qn.T.astype(BF16)
    q_ref[0, :, rows] = q_t

    nb = km_all.shape[0]
    km = km_all[...].astype(BF16)
    feat = lax.broadcasted_iota(jnp.int32, (V7X_LANES, 1), 0)
    for head in range(ATTN_HEADS):
        p, h = divmod(head, HEADS_PER_PAIR)
        q_pair = q_t[p * V7X_LANES:(p + 1) * V7X_LANES]
        qh = jnp.where((feat >= h * HEAD_DIM) & (feat < (h + 1) * HEAD_DIM), q_pair, jnp.zeros_like(q_pair))
        gate = _dot(km[:, p * V7X_LANES:(p + 1) * V7X_LANES], qh)
        bias_ref[0, head * nb:(head + 1) * nb, rows] = _select_bias(gate, blk_id).astype(BF16)

    v_ref[0, :, rows] = proj(2).T.astype(BF16)


def _lru_side(sub, hb, w_in_ref, cw_ref, cb_ref, wg_ref, bg_ref, lam_ref, go_ref, lru_ref, conv_carry, h_carry,
              scan_buf):
    tm = MOBA_BLOCK
    rows = slice(sub * tm, (sub + 1) * tm)
    proj = functools.partial(_in_proj, hb, w_in_ref)

    xc = _causal_conv_run_major(_to_run_major(scan_buf.at[2 * sub], proj(3)), cw_ref, cb_ref, conv_carry)
    xcb = xc.astype(BF16)
    pre = [_dot(xcb[:, j * V7X_MXU_DIM:(j + 1) * V7X_MXU_DIM], wg_ref[j]) for j in range(LRU_WIDTH // V7X_MXU_DIM)]
    bg = bg_ref[...]
    r = jax.nn.sigmoid(jnp.concatenate([g[:, :V7X_MXU_DIM] for g in pre], axis=1) + bg[:, :LRU_WIDTH])
    i = jax.nn.sigmoid(jnp.concatenate([g[:, V7X_MXU_DIM:] for g in pre], axis=1) + bg[:, LRU_WIDTH:])
    neg_lam = -lam_ref[...]
    softplus = jnp.maximum(neg_lam, 0.0) + jnp.log1p(jnp.exp(-jnp.abs(neg_lam)))
    log_a = r * ((-LRU_C) * softplus)
    a = jnp.exp(log_a)
    th = jnp.tanh(log_a)
    u = jnp.sqrt(-2.0 * th / (1.0 - th)) * (i * xc)
    hs, h_last = _linear_scan_run_major(a, u, h_carry[0:1, :])
    h_carry[0:1, :] = h_last
    hs = _from_run_major(scan_buf.at[2 * sub + 1], hs)

    y = hs * _gelu_tanh(proj(4))
    lru_ref[0, rows, :] = (y * _rms_scale(y) * go_ref[...]).astype(BF16)


def _mixer_in(x, mod, g1, w_in, gq, gk, gmat, cw, cb, wg, bg, lam, go):
    B, S, _ = x.shape
    tm = TM_IN
    assert tm % MOBA_BLOCK == 0, "a grid step covers whole MoBA blocks: the block choice reads the means of earlier blocks"
    nb = S // MOBA_BLOCK
    const = lambda shape: pl.BlockSpec(shape, lambda b, s: (0,) * len(shape))
    tok = lambda w: pl.BlockSpec((1, tm, w), lambda b, s: (b, s, 0))
    feat_major = pl.BlockSpec((1, ATTN_WIDTH, tm), lambda b, s: (b, 0, s))
    return pl.pallas_call(
        _mixer_in_kernel,
        grid=(B, S // tm),
        in_specs=[
            tok(D_MODEL),
            pl.BlockSpec((1, N_MOD, D_MODEL), lambda b, s: (b, 0, 0)),
            const((1, D_MODEL)),
            const((D_MODEL, N_IN)),
            const((1, ATTN_WIDTH)),
            const((1, ATTN_WIDTH)),
            const((V7X_MXU_DIM, V7X_MXU_DIM)),
            const((LRU_CONV, LRU_WIDTH)),
            const((1, LRU_WIDTH)),
            const((LRU_WIDTH // V7X_MXU_DIM, V7X_MXU_DIM, 2 * V7X_MXU_DIM)),
            const((1, 2 * LRU_WIDTH)),
            const((1, LRU_WIDTH)),
            const((1, LRU_WIDTH)),
        ],
        out_specs=[
            feat_major,
            tok(ATTN_WIDTH),
            feat_major,
            pl.BlockSpec((1, ATTN_HEADS * nb, tm), lambda b, s: (b, 0, s)),
            tok(LRU_WIDTH),
        ],
        out_shape=[
            jax.ShapeDtypeStruct((B, ATTN_WIDTH, S), BF16),
            jax.ShapeDtypeStruct((B, S, ATTN_WIDTH), BF16),
            jax.ShapeDtypeStruct((B, ATTN_WIDTH, S), BF16),
            jax.ShapeDtypeStruct((B, ATTN_HEADS * nb, S), BF16),
            jax.ShapeDtypeStruct((B, S, LRU_WIDTH), BF16),
        ],
        scratch_shapes=[
            pltpu.VMEM((V7X_SUBLANES, LRU_WIDTH), F32),
            pltpu.VMEM((V7X_SUBLANES, LRU_WIDTH), F32),
            pltpu.VMEM((nb, ATTN_WIDTH), F32),
            pltpu.VMEM((2 * (tm // MOBA_BLOCK), LRU_WIDTH // V7X_LANES, MOBA_BLOCK, V7X_LANES), F32),
        ],
        compiler_params=pltpu.CompilerParams(
            dimension_semantics=("arbitrary", "arbitrary"), vmem_limit_bytes=V7X_VMEM_LIMIT_BYTES),
        name="mixer_in",
    )(x, mod, g1, w_in, gq, gk, gmat, cw, cb, wg, bg, lam, go)


def _select_bias(gate, n_past):
    nb = gate.shape[0]
    blk = lax.broadcasted_iota(jnp.int32, gate.shape, 0)
    neg_inf = jnp.float32(-jnp.inf)
    g = jnp.where(blk < n_past, gate, neg_inf)
    bias = jnp.full(gate.shape, MASK_NEG, F32)
    for _ in range(MOBA_TOPK):
        mx = jnp.max(g, axis=0, keepdims=True)
        cand = jnp.where((g == mx) & (mx > neg_inf), blk, nb)
        pick = blk == jnp.min(cand, axis=0, keepdims=True)
        bias = jnp.where(pick, 0.0, bias)
        g = jnp.where(pick, neg_inf, g)
    return bias


def _moba_kernel(q_ref, k_ref, v_ref, bias_ref, blk_ref, o_ref, s_even, s_odd, mx_even, mx_odd):
    def tile(i, _):
        _moba_tile(i, q_ref, k_ref, v_ref, bias_ref, blk_ref, o_ref, s_even, s_odd, mx_even, mx_odd)
        return 0

    lax.fori_loop(0, q_ref.shape[2] // (MOBA_TILE_BLOCKS * MOBA_BLOCK), tile, 0)


def _moba_tile(it, q_ref, k_ref, v_ref, bias_ref, blk_ref, o_ref, s_even, s_odd, mx_even, mx_odd):
    tq = MOBA_TILE_BLOCKS * MOBA_BLOCK
    nb = bias_ref.shape[1] // HEADS_PER_PAIR
    feat = lax.broadcasted_iota(jnp.int32, (V7X_LANES, 1), 0)
    start = pl.multiple_of(it * tq, tq)
    i = it * MOBA_TILE_BLOCKS + (MOBA_TILE_BLOCKS - 1)
    q_t = q_ref[0, :, pl.ds(start, tq)]

    def v_ext(h, start, n):
        vh = v_ref[0, h * HEAD_DIM:(h + 1) * HEAD_DIM, pl.ds(start, n)]
        return jnp.concatenate([vh, jnp.ones((2 * V7X_SUBLANES, n), BF16)], axis=0)

    key_pos = lax.broadcasted_iota(jnp.int32, (MOBA_BLOCK, MOBA_BLOCK), 0)
    qry_pos = lax.broadcasted_iota(jnp.int32, (MOBA_BLOCK, MOBA_BLOCK), 1)
    feat_pad = jnp.zeros((V7X_LANES - nb, tq), BF16)

    q_heads = [jnp.where((feat >= h * HEAD_DIM) & (feat < (h + 1) * HEAD_DIM), q_t, jnp.zeros_like(q_t))
               for h in range(HEADS_PER_PAIR)]
    qx = jnp.concatenate(
        [jnp.concatenate([q_heads[h], bias_ref[0, h * nb:(h + 1) * nb, pl.ds(start, tq)], feat_pad], axis=0)
         for h in range(HEADS_PER_PAIR)], axis=1)

    ck = KV_CHUNK_BLOCKS * MOBA_BLOCK
    n_chunks = (i + KV_CHUNK_BLOCKS - 1) // KV_CHUNK_BLOCKS

    def block_scores(c, t):
        ks = pl.multiple_of(c * ck + t * MOBA_BLOCK, MOBA_BLOCK)
        kx = jnp.concatenate([k_ref[0, pl.ds(ks, MOBA_BLOCK), :], blk_ref[pl.ds(ks, MOBA_BLOCK), :]], axis=1)
        return _dot(kx, qx)

    def run_step(c, cur, nxt, carry, n_cur=KV_CHUNK_BLOCKS, n_nxt=KV_CHUNK_BLOCKS):
        s_ref, mx_ref = cur
        m = [carry[2 * h] for h in range(HEADS_PER_PAIR)]
        acc = [carry[2 * h + 1] for h in range(HEADS_PER_PAIR)]
        if nxt is None:
            n_nxt = 0
        for t in range(max(n_cur, n_nxt)):
            rows = slice(t * MOBA_BLOCK, (t + 1) * MOBA_BLOCK)
            if t < n_nxt:
                s_nxt = block_scores(c + 1, t)
                nxt[0][rows, :] = s_nxt
                nxt[1][t:t + 1, :] = jnp.max(s_nxt, axis=0, keepdims=True)
            if t >= n_cur:
                continue
            ks = pl.multiple_of(c * ck + t * MOBA_BLOCK, MOBA_BLOCK)
            for h in range(HEADS_PER_PAIR):
                cols = slice(h * tq, (h + 1) * tq)
                m_new = jnp.maximum(m[h], mx_ref[t:t + 1, cols])
                p = jnp.exp(s_ref[rows, cols] - m_new).astype(BF16)
                acc[h] = jnp.exp(m[h] - m_new) * acc[h] + _dot(v_ext(h, ks, MOBA_BLOCK), p)
                m[h] = m_new
        return (m[0], acc[0], m[1], acc[1])

    even, odd = (s_even, mx_even), (s_odd, mx_odd)

    own = [(h, u) for h in range(HEADS_PER_PAIR) for u in range(MOBA_TILE_BLOCKS)]
    s_own = {}
    own_start = [pl.multiple_of(start + u * MOBA_BLOCK, MOBA_BLOCK) for u in range(MOBA_TILE_BLOCKS)]
    for h, u in own:
        kd = k_ref[0, pl.ds(own_start[u], MOBA_BLOCK), :]
        s = _dot(kd, q_heads[h][:, u * MOBA_BLOCK:(u + 1) * MOBA_BLOCK])
        s_own[h, u] = jnp.where(key_pos <= qry_pos, s, MASK_NEG)
    for t in range(KV_CHUNK_BLOCKS):
        s_first = block_scores(0, t)
        s_even[t * MOBA_BLOCK:(t + 1) * MOBA_BLOCK, :] = s_first
        mx_even[t:t + 1, :] = jnp.max(s_first, axis=0, keepdims=True)
    carry = []
    for h in range(HEADS_PER_PAIR):
        m_own, acc_own = [], []
        for u in range(MOBA_TILE_BLOCKS):
            m = jnp.max(s_own[h, u], axis=0, keepdims=True)
            p = jnp.exp(s_own[h, u] - m).astype(BF16)
            m_own.append(m)
            acc_own.append(_dot(v_ext(h, own_start[u], MOBA_BLOCK), p))
        carry += [jnp.concatenate(m_own, axis=1), jnp.concatenate(acc_own, axis=1)]

    def pair(k, carry):
        carry = run_step(2 * k, even, odd, carry)
        return run_step(2 * k + 1, odd, even, carry)

    carry = lax.fori_loop(0, jnp.maximum(n_chunks - 1, 0) // 2, pair, tuple(carry))

    last_blocks = i - (n_chunks - 1) * KV_CHUNK_BLOCKS

    def last_one(r):
        return lambda: finish(run_step(n_chunks - 1, even, None, carry, n_cur=r))

    def last_two(r):
        def run():
            mid = run_step(n_chunks - 2, even, odd, carry, n_nxt=r)
            return finish(run_step(n_chunks - 1, odd, None, mid, n_cur=r))
        return run

    counts = range(1, KV_CHUNK_BLOCKS + 1)

    def finish(carry):
        heads = []
        for h in range(HEADS_PER_PAIR):
            acc = carry[2 * h + 1]
            heads.append(acc[0:HEAD_DIM] / acc[HEAD_DIM:HEAD_DIM + 1])
        o_ref[0, pl.ds(start, tq), :] = jnp.concatenate(heads, axis=0).T
        return 0

    lax.cond(
        n_chunks % 2 == 1,
        lambda: lax.switch(last_blocks - 1, [last_one(r) for r in counts]),
        lambda: lax.cond(n_chunks > 0,
                         lambda: lax.switch(last_blocks - 1, [last_two(r) for r in counts]),
                         lambda: finish(carry)))


def _moba(q, k, v, bias):
    B, S, _ = k.shape
    nb = S // MOBA_BLOCK
    tq = MOBA_TILE_BLOCKS * MOBA_BLOCK
    ck = KV_CHUNK_BLOCKS * MOBA_BLOCK
    block_onehot = (jnp.arange(S)[:, None] // MOBA_BLOCK == jnp.arange(V7X_LANES)[None, :]).astype(BF16)
    feat_major = pl.BlockSpec((1, V7X_LANES, S), lambda b, p: (b, p, 0))
    return pl.pallas_call(
        _moba_kernel,
        grid=(B, N_PAIRS),
        in_specs=[
            feat_major,
            pl.BlockSpec((1, S, V7X_LANES), lambda b, p: (b, 0, p)),
            feat_major,
            pl.BlockSpec((1, HEADS_PER_PAIR * nb, S), lambda b, p: (b, p, 0)),
            pl.BlockSpec((S, V7X_LANES), lambda b, p: (0, 0), pipeline_mode=pl.Buffered(1)),
        ],
        out_specs=pl.BlockSpec((1, S, V7X_LANES), lambda b, p: (b, 0, p)),
        out_shape=jax.ShapeDtypeStruct((B, S, ATTN_WIDTH), F32),
        scratch_shapes=[
            pltpu.VMEM((ck, HEADS_PER_PAIR * tq), F32),
            pltpu.VMEM((ck, HEADS_PER_PAIR * tq), F32),
            pltpu.VMEM((V7X_SUBLANES, HEADS_PER_PAIR * tq), F32),
            pltpu.VMEM((V7X_SUBLANES, HEADS_PER_PAIR * tq), F32),
        ],
        compiler_params=pltpu.CompilerParams(
            dimension_semantics=("arbitrary", "arbitrary"),
            vmem_limit_bytes=V7X_VMEM_LIMIT_BYTES),
        name="moba",
    )(q, k, v, bias, block_onehot)


def _mixer_out_kernel(x_ref, lru_ref, attn_ref, mod_ref, ga_ref, w_out_ref, g2_ref, w_up_ref, cw_ref, cb_ref,
                      w_down_ref, o_ref, conv_carry, perm_in, perm_out):
    tm = x_ref.shape[1]

    @pl.when(pl.program_id(1) == 0)
    def _():
        conv_carry[...] = jnp.zeros_like(conv_carry)

    mod = mod_ref[0]
    gate1, sh2, sc2, gate2 = mod[2:3, :], mod[3:4, :], mod[4:5, :], mod[5:6, :]

    gain2 = g2_ref[...] * (1.0 + sc2)
    x1_parts, h2_parts = [], []
    for r0 in range(0, tm, tm // 2):
        rows = slice(r0, r0 + tm // 2)
        attn = attn_ref[0, rows, :]
        attn_n = (attn * _rms_scale(attn) * ga_ref[...]).astype(BF16)
        mixed = _dot(lru_ref[0, rows, :], w_out_ref[0:LRU_WIDTH, :]) + _dot(attn_n, w_out_ref[LRU_WIDTH:, :])
        x1_parts.append(x_ref[0, rows, :] + gate1 * mixed)
        h2_parts.append((x1_parts[-1] * _rms_scale(x1_parts[-1])) * gain2 + sh2)
    x1 = jnp.concatenate(x1_parts, axis=0)

    h2 = _to_run_major(perm_in, jnp.concatenate(h2_parts, axis=0)).astype(BF16)

    def conv_cols(c0, width):
        cols = slice(c0, c0 + width)
        return _causal_conv_run_major(_dot(h2, w_up_ref[:, cols]), cw_ref, cb_ref, conv_carry, cols)

    acc = None
    c0 = 0
    for width in FF_CHUNKS:
        g = conv_cols(c0, width)
        val = conv_cols(D_FF + c0, width)
        act = (g * jax.nn.sigmoid(g) * val).astype(BF16)
        part = _dot(act, w_down_ref[c0:c0 + width, :])
        acc = part if acc is None else acc + part
        c0 += width

    o_ref[0] = x1 + gate2 * _from_run_major(perm_out, acc)


def _mixer_out(x, lru, attn, mod, ga, w_out, g2, w_up, cw, cb, w_down):
    B, S, _ = x.shape
    tm = TM_OUT
    const = lambda shape: pl.BlockSpec(shape, lambda b, s: (0,) * len(shape), pipeline_mode=pl.Buffered(1))
    tok = lambda w: pl.BlockSpec((1, tm, w), lambda b, s: (b, s, 0))
    return pl.pallas_call(
        _mixer_out_kernel,
        grid=(B, S // tm),
        in_specs=[
            tok(D_MODEL),
            tok(LRU_WIDTH),
            tok(ATTN_WIDTH),
            pl.BlockSpec((1, N_MOD, D_MODEL), lambda b, s: (b, 0, 0)),
            const((1, ATTN_WIDTH)),
            const((D_MODEL, D_MODEL)),
            const((1, D_MODEL)),
            const((D_MODEL, 2 * D_FF)),
            const((FFN_CONV, 2 * D_FF)),
            const((1, 2 * D_FF)),
            const((D_FF, D_MODEL)),
        ],
        out_specs=tok(D_MODEL),
        out_shape=jax.ShapeDtypeStruct((B, S, D_MODEL), F32),
        scratch_shapes=[
            pltpu.VMEM((V7X_SUBLANES, 2 * D_FF), F32),
            pltpu.VMEM((D_MODEL // V7X_LANES, tm, V7X_LANES), F32),
            pltpu.VMEM((D_MODEL // V7X_LANES, tm, V7X_LANES), F32),
        ],
        compiler_params=pltpu.CompilerParams(
            dimension_semantics=("arbitrary", "arbitrary"), vmem_limit_bytes=V7X_VMEM_LIMIT_BYTES),
        name="mixer_out",
    )(x, lru, attn, mod, ga, w_out, g2, w_up, cw, cb, w_down)


def _block_diag(w):
    n, d, _ = w.shape
    eye = jnp.eye(n, dtype=w.dtype)
    return jnp.einsum('hij,hg->higj', w, eye).reshape(n * d, n * d)


def _layer(x, c, w_ada, b_ada, norm1_g, w_in, q_norm_g, k_norm_g, lru_conv_w, lru_conv_b, lru_wa, lru_ba,
           lru_wx, lru_bx, lru_lambda, lru_out_g, attn_out_g, w_out, norm2_g, w_up, ffn_conv_w, ffn_conv_b,
           w_down):
    B, S, _ = x.shape
    row = lambda t: t.reshape(1, -1)
    mod = _modulation(c, w_ada, row(b_ada)).reshape(B, N_MOD, D_MODEL)

    head_id = jnp.arange(V7X_MXU_DIM) // HEAD_DIM
    gmat = jnp.where(head_id[:, None] == head_id[None, :], 1.0 / HEAD_DIM, 0.0).astype(BF16)
    heads_per_tile = V7X_MXU_DIM // (LRU_WIDTH // LRU_HEADS)
    w_gates = jnp.stack([
        jnp.concatenate([_block_diag(w[j * heads_per_tile:(j + 1) * heads_per_tile]) for w in (lru_wa, lru_wx)], axis=1)
        for j in range(LRU_HEADS // heads_per_tile)]).astype(BF16)
    b_gates = jnp.concatenate([lru_ba, lru_bx]).reshape(1, -1)

    q, k, v, bias, lru = _mixer_in(
        x, mod, row(norm1_g), w_in.astype(BF16), row(jnp.tile(q_norm_g, ATTN_HEADS)),
        row(jnp.tile(k_norm_g, ATTN_HEADS)), gmat, lru_conv_w, row(lru_conv_b), w_gates, b_gates,
        row(lru_lambda), row(lru_out_g))
    attn = _moba(q, k, v, bias)
    return _mixer_out(x, lru, attn, mod, row(attn_out_g), w_out.astype(BF16), row(norm2_g), w_up.astype(BF16),
                      ffn_conv_w, row(ffn_conv_b), w_down.astype(BF16))


def kernel(x, c, w_ada, b_ada, norm1_g, w_in, q_norm_g, k_norm_g, lru_conv_w, lru_conv_b, lru_wa, lru_ba, lru_wx, lru_bx, lru_lambda, lru_out_g, attn_out_g, w_out, norm2_g, w_up, ffn_conv_w, ffn_conv_b, w_down):
    depth = w_ada.shape[0]
    for l in range(depth):
        x = _layer(x, c, w_ada[l], b_ada[l], norm1_g[l], w_in[l], q_norm_g[l], k_norm_g[l], lru_conv_w[l],
                   lru_conv_b[l], lru_wa[l], lru_ba[l], lru_wx[l], lru_bx[l], lru_lambda[l], lru_out_g[l],
                   attn_out_g[l], w_out[l], norm2_g[l], w_up[l], ffn_conv_w[l], ffn_conv_b[l], w_down[l])
    return x
```

```python
import functools
import math

import jax
import jax.numpy as jnp
from jax import lax
from jax.experimental import pallas as pl
from jax.experimental.pallas import tpu as pltpu

D_MODEL = 1024
ATTN_HEADS = 8
HEAD_DIM = 64
ATTN_WIDTH = ATTN_HEADS * HEAD_DIM
MOBA_BLOCK = 256
MOBA_TOPK = 3
LRU_WIDTH = D_MODEL - ATTN_WIDTH
LRU_HEADS = 8
LRU_CONV = 4
LRU_C = 8.0
N_IN = 3 * ATTN_WIDTH + 2 * LRU_WIDTH
D_FF = 2816
FFN_CONV = 3
N_MOD = 6
EPS = 1e-6

V7X_LANES = 128
V7X_SUBLANES = 8
V7X_MXU_DIM = 256
V7X_BF16_ROWS_PER_VREG = 16
V7X_VMEM_LIMIT_BYTES = 56 * 1024 * 1024

HEADS_PER_PAIR = V7X_LANES // HEAD_DIM
N_PAIRS = ATTN_HEADS // HEADS_PER_PAIR
MASK_NEG = -1e30

TM_IN = 1024
TM_OUT = 512
KV_CHUNK_BLOCKS = 4
MOD_CHUNK = 1536

BF16 = jnp.bfloat16
F32 = jnp.float32


def _dot(a, b):
    return jnp.dot(a, b, preferred_element_type=F32)


def _rms_scale(y):
    return lax.rsqrt(jnp.mean(y * y, axis=-1, keepdims=True) + EPS)


def _mod_kernel(c_ref, w_ref, b_ref, o_ref):
    o_ref[...] = _dot(c_ref[...], w_ref[...]) + b_ref[...]


def _modulation(c, w_ada, b_ada):
    B = c.shape[0]
    n = w_ada.shape[1]
    return pl.pallas_call(
        _mod_kernel,
        grid=(n // MOD_CHUNK,),
        in_specs=[
            pl.BlockSpec((B, D_MODEL), lambda j: (0, 0)),
            pl.BlockSpec((D_MODEL, MOD_CHUNK), lambda j: (0, j)),
            pl.BlockSpec((1, MOD_CHUNK), lambda j: (0, j)),
        ],
        out_specs=pl.BlockSpec((B, MOD_CHUNK), lambda j: (0, j)),
        out_shape=jax.ShapeDtypeStruct((B, n), F32),
        compiler_params=pltpu.CompilerParams(
            dimension_semantics=("arbitrary",), vmem_limit_bytes=V7X_VMEM_LIMIT_BYTES),
        name="modulation",
    )(c, w_ada, b_ada)


def _head_rmsnorm(t, gmat, gain):
    sq = t * t
    hi = sq.astype(BF16)
    lo = (sq - hi.astype(F32)).astype(BF16)
    ms = jnp.concatenate(
        [_dot(hi[:, c:c + V7X_MXU_DIM], gmat) + _dot(lo[:, c:c + V7X_MXU_DIM], gmat)
         for c in range(0, t.shape[1], V7X_MXU_DIM)], axis=1)
    return t * lax.rsqrt(ms + EPS) * gain


def _shift_rows(t, d, fill):
    row = lax.broadcasted_iota(jnp.int32, t.shape, 0)
    return jnp.where(row >= d, pltpu.roll(t, d, 0), fill)


SCAN_RUN = V7X_SUBLANES
SCAN_GROUP = SCAN_RUN * V7X_SUBLANES


def _to_run_major(buf, t):
    tm, width = t.shape
    lane_groups = width // V7X_LANES
    for l in range(lane_groups):
        buf[l] = t[:, l * V7X_LANES:(l + 1) * V7X_LANES]
    pieces = []
    for g in range(tm // SCAN_GROUP):
        for s in range(SCAN_RUN):
            rows = pl.ds(g * SCAN_GROUP + s, V7X_SUBLANES, stride=SCAN_RUN)
            pieces.append(jnp.concatenate([buf[l, rows, :] for l in range(lane_groups)], axis=1))
    return jnp.concatenate(pieces, axis=0)


def _from_run_major(buf, t):
    tm, width = t.shape
    lane_groups = width // V7X_LANES
    for g in range(tm // SCAN_GROUP):
        for s in range(SCAN_RUN):
            piece = t[g * SCAN_GROUP + s * V7X_SUBLANES:g * SCAN_GROUP + (s + 1) * V7X_SUBLANES]
            rows = pl.ds(g * SCAN_GROUP + s, V7X_SUBLANES, stride=SCAN_RUN)
            for l in range(lane_groups):
                buf[l, rows, :] = piece[:, l * V7X_LANES:(l + 1) * V7X_LANES]
    return jnp.concatenate([buf[l] for l in range(lane_groups)], axis=1)


def _causal_conv_run_major(xp, w_ref, b_ref, carry_ref, cols=slice(None)):
    taps = w_ref.shape[0]
    tm, width = xp.shape
    row = lax.broadcasted_iota(jnp.int32, (V7X_SUBLANES, width), 0)
    wrapped = range(SCAN_RUN - (taps - 1), SCAN_RUN)
    above = {s: carry_ref[s:s + 1, cols] for s in wrapped}
    bias = b_ref[:, cols]
    w = [w_ref[k:k + 1, cols] for k in range(taps)]
    out = []
    for g in range(tm // SCAN_GROUP):
        def piece(s):
            return xp[g * SCAN_GROUP + s * V7X_SUBLANES:g * SCAN_GROUP + (s + 1) * V7X_SUBLANES]

        prev_run = {}
        for s in wrapped:
            prev_run[s] = jnp.where(row >= 1, pltpu.roll(piece(s), 1, 0), above[s])
            above[s] = piece(s)[V7X_SUBLANES - 1:V7X_SUBLANES]
        for s in range(SCAN_RUN):
            acc = bias + w[taps - 1] * piece(s)
            for d in range(1, taps):
                src = piece(s - d) if s >= d else prev_run[s - d + SCAN_RUN]
                acc = acc + w[taps - 1 - d] * src
            out.append(acc)
    for s in wrapped:
        carry_ref[s:s + 1, cols] = above[s]
    return jnp.concatenate(out, axis=0)


def _linear_scan_run_major(a, u, h_in):
    tm = a.shape[0]
    row = lax.broadcasted_iota(jnp.int32, (V7X_SUBLANES, a.shape[1]), 0)
    out = []
    for g in range(tm // SCAN_GROUP):
        h_loc, a_run = [], []
        for s in range(SCAN_RUN):
            rows = slice(g * SCAN_GROUP + s * V7X_SUBLANES, g * SCAN_GROUP + (s + 1) * V7X_SUBLANES)
            h_loc.append(u[rows] if s == 0 else a[rows] * h_loc[-1] + u[rows])
            a_run.append(a[rows] if s == 0 else a[rows] * a_run[-1])
        h_end, a_end = h_loc[-1], a_run[-1]
        d = 1
        while d < V7X_SUBLANES:
            h_end = a_end * _shift_rows(h_end, d, 0.0) + h_end
            a_end = a_end * _shift_rows(a_end, d, 1.0)
            d *= 2
        after = h_end + a_end * h_in
        before = jnp.where(row >= 1, pltpu.roll(after, 1, 0), h_in)
        h_in = after[V7X_SUBLANES - 1:V7X_SUBLANES]
        out += [h_loc[s] + a_run[s] * before for s in range(SCAN_RUN)]
    return jnp.concatenate(out, axis=0), h_in


def _gelu_tanh(t):
    return 0.5 * t * (1.0 + jnp.tanh(math.sqrt(2.0 / math.pi) * (t + 0.044715 * (t * t * t))))


def _mixer_in_kernel(x_ref, mod_ref, g1_ref, w_in_ref, gq_ref, gk_ref, gmat_ref, cw_ref, cb_ref,
                     wg_ref, bg_ref, lam_ref, go_ref,
                     q_ref, k_ref, v_ref, bias_ref, lru_ref,
                     conv_carry, h_carry, km_all, scan_buf):
    @pl.when(pl.program_id(1) == 0)
    def _():
        conv_carry[...] = jnp.zeros_like(conv_carry)
        h_carry[...] = jnp.zeros_like(h_carry)
        km_all[...] = jnp.zeros_like(km_all)

    blocks_per_step = x_ref.shape[1] // MOBA_BLOCK
    mod = mod_ref[0]
    gain = g1_ref[...] * (1.0 + mod[1:2, :])
    shift = mod[0:1, :]
    normed = []
    for sub in range(blocks_per_step):
        x = x_ref[0, sub * MOBA_BLOCK:(sub + 1) * MOBA_BLOCK, :]
        normed.append(((x * _rms_scale(x)) * gain + shift).astype(BF16))
    for sub in range(blocks_per_step):
        _attention_side(sub, pl.program_id(1) * blocks_per_step + sub, normed[sub], w_in_ref, gq_ref, gk_ref,
                        gmat_ref, q_ref, k_ref, v_ref, bias_ref, km_all)
    for sub in range(blocks_per_step):
        _lru_side(sub, normed[sub], w_in_ref, cw_ref, cb_ref, wg_ref, bg_ref, lam_ref, go_ref, lru_ref,
                  conv_carry, h_carry, scan_buf)


def _in_proj(hb, w_in_ref, j):
    return _dot(hb, w_in_ref[:, j * ATTN_WIDTH:(j + 1) * ATTN_WIDTH])


def _attention_side(sub, blk_id, hb, w_in_ref, gq_ref, gk_ref, gmat_ref, q_ref, k_ref, v_ref, bias_ref, km_all):
    tm = MOBA_BLOCK
    rows = slice(sub * tm, (sub + 1) * tm)
    proj = functools.partial(_in_proj, hb, w_in_ref)
    gmat = gmat_ref[...]

    kn = _head_rmsnorm(proj(1), gmat, gk_ref[...])
    k_ref[0, rows, :] = kn.astype(BF16)
    km_all[pl.ds(blk_id, 1), :] = jnp.mean(kn, axis=0, keepdims=True)

    qn =_head_rmsnorm(proj(0), gmat, gq_ref[...] * (HEAD_DIM ** -0.5))
    q_t = qn.T.astype(BF16)
    q_ref[0, :, rows] = q_t

    nb = km_all.shape[0]
    km = km_all[...].astype(BF16)
    feat = lax.broadcasted_iota(jnp.int32, (V7X_LANES, 1), 0)
    for head in range(ATTN_HEADS):
        p, h = divmod(head, HEADS_PER_PAIR)
        q_pair = q_t[p * V7X_LANES:(p + 1) * V7X_LANES]
        qh = jnp.where((feat >= h * HEAD_DIM) & (feat < (h + 1) * HEAD_DIM), q_pair, jnp.zeros_like(q_pair))
        gate = _dot(km[:, p * V7X_LANES:(p + 1) * V7X_LANES], qh)
        bias_ref[0, head * nb:(head + 1) * nb, rows] = _select_bias(gate, blk_id).astype(BF16)

    v_ref[0, :, rows] = proj(2).T.astype(BF16)


def _lru_side(sub, hb, w_in_ref, cw_ref, cb_ref, wg_ref, bg_ref, lam_ref, go_ref, lru_ref, conv_carry, h_carry,
              scan_buf):
    tm = MOBA_BLOCK
    rows = slice(sub * tm, (sub + 1) * tm)
    proj = functools.partial(_in_proj, hb, w_in_ref)

    xc = _causal_conv_run_major(_to_run_major(scan_buf.at[2 * sub], proj(3)), cw_ref, cb_ref, conv_carry)
    xcb = xc.astype(BF16)
    pre = [_dot(xcb[:, j * V7X_MXU_DIM:(j + 1) * V7X_MXU_DIM], wg_ref[j]) for j in range(LRU_WIDTH // V7X_MXU_DIM)]
    bg = bg_ref[...]
    r = jax.nn.sigmoid(jnp.concatenate([g[:, :V7X_MXU_DIM] for g in pre], axis=1) + bg[:, :LRU_WIDTH])
    i = jax.nn.sigmoid(jnp.concatenate([g[:, V7X_MXU_DIM:] for g in pre], axis=1) + bg[:, LRU_WIDTH:])
    neg_lam = -lam_ref[...]
    softplus = jnp.maximum(neg_lam, 0.0) + jnp.log1p(jnp.exp(-jnp.abs(neg_lam)))
    log_a = r * ((-LRU_C) * softplus)
    a = jnp.exp(log_a)
    th = jnp.tanh(log_a)
    u = jnp.sqrt(-2.0 * th / (1.0 - th)) * (i * xc)
    hs, h_last = _linear_scan_run_major(a, u, h_carry[0:1, :])
    h_carry[0:1, :] = h_last
    hs = _from_run_major(scan_buf.at[2 * sub + 1], hs)

    y = hs * _gelu_tanh(proj(4))
    lru_ref[0, rows, :] = (y * _rms_scale(y) * go_ref[...]).astype(BF16)


def _mixer_in(x, mod, g1, w_in, gq, gk, gmat, cw, cb, wg, bg, lam, go):
    B, S, _ = x.shape
    tm = TM_IN
    assert tm % MOBA_BLOCK == 0, "a grid step covers whole MoBA blocks: the block choice reads the means of earlier blocks"
    nb = S // MOBA_BLOCK
    const = lambda shape: pl.BlockSpec(shape, lambda b, s: (0,) * len(shape))
    tok = lambda w: pl.BlockSpec((1, tm, w), lambda b, s: (b, s, 0))
    feat_major = pl.BlockSpec((1, ATTN_WIDTH, tm), lambda b, s: (b, 0, s))
    return pl.pallas_call(
        _mixer_in_kernel,
        grid=(B, S // tm),
        in_specs=[
            tok(D_MODEL),
            pl.BlockSpec((1, N_MOD, D_MODEL), lambda b, s: (b, 0, 0)),
            const((1, D_MODEL)),
            const((D_MODEL, N_IN)),
            const((1, ATTN_WIDTH)),
            const((1, ATTN_WIDTH)),
            const((V7X_MXU_DIM, V7X_MXU_DIM)),
            const((LRU_CONV, LRU_WIDTH)),
            const((1, LRU_WIDTH)),
            const((LRU_WIDTH // V7X_MXU_DIM, V7X_MXU_DIM, 2 * V7X_MXU_DIM)),
            const((1, 2 * LRU_WIDTH)),
            const((1, LRU_WIDTH)),
            const((1, LRU_WIDTH)),
        ],
        out_specs=[
            feat_major,
            tok(ATTN_WIDTH),
            feat_major,
            pl.BlockSpec((1, ATTN_HEADS * nb, tm), lambda b, s: (b, 0, s)),
            tok(LRU_WIDTH),
        ],
        out_shape=[
            jax.ShapeDtypeStruct((B, ATTN_WIDTH, S), BF16),
            jax.ShapeDtypeStruct((B, S, ATTN_WIDTH), BF16),
            jax.ShapeDtypeStruct((B, ATTN_WIDTH, S), BF16),
            jax.ShapeDtypeStruct((B, ATTN_HEADS * nb, S), BF16),
            jax.ShapeDtypeStruct((B, S, LRU_WIDTH), BF16),
        ],
        scratch_shapes=[
            pltpu.VMEM((V7X_SUBLANES, LRU_WIDTH), F32),
            pltpu.VMEM((V7X_SUBLANES, LRU_WIDTH), F32),
            pltpu.VMEM((nb, ATTN_WIDTH), F32),
            pltpu.VMEM((2 * (tm // MOBA_BLOCK), LRU_WIDTH // V7X_LANES, MOBA_BLOCK, V7X_LANES), F32),
        ],
        compiler_params=pltpu.CompilerParams(
            dimension_semantics=("arbitrary", "arbitrary"), vmem_limit_bytes=V7X_VMEM_LIMIT_BYTES),
        name="mixer_in",
    )(x, mod, g1, w_in, gq, gk, gmat, cw, cb, wg, bg, lam, go)


def _select_bias(gate, n_past):
    nb = gate.shape[0]
    blk = lax.broadcasted_iota(jnp.int32, gate.shape, 0)
    neg_inf = jnp.float32(-jnp.inf)
    g = jnp.where(blk < n_past, gate, neg_inf)
    bias = jnp.full(gate.shape, MASK_NEG, F32)
    for _ in range(MOBA_TOPK):
        mx = jnp.max(g, axis=0, keepdims=True)
        cand = jnp.where((g == mx) & (mx > neg_inf), blk, nb)
        pick = blk == jnp.min(cand, axis=0, keepdims=True)
        bias = jnp.where(pick, 0.0, bias)
        g = jnp.where(pick, neg_inf, g)
    return bias


def _moba_kernel(q_ref, k_ref, v_ref, bias_ref, blk_ref, o_ref, s_even, s_odd, mx_even, mx_odd):
    def tile(i, _):
        _moba_tile(i, q_ref, k_ref, v_ref, bias_ref, blk_ref, o_ref, s_even, s_odd, mx_even, mx_odd)
        return 0

    lax.fori_loop(0, q_ref.shape[2] // MOBA_BLOCK, tile, 0)


def _moba_tile(i, q_ref, k_ref, v_ref, bias_ref, blk_ref, o_ref, s_even, s_odd, mx_even, mx_odd):
    tq = MOBA_BLOCK
    nb = bias_ref.shape[1] // HEADS_PER_PAIR
    feat = lax.broadcasted_iota(jnp.int32, (V7X_LANES, 1), 0)
    start = pl.multiple_of(i * tq, tq)
    q_t = q_ref[0, :, pl.ds(start, tq)]

    def v_ext(h, start, n):
        vh = v_ref[0, h * HEAD_DIM:(h + 1) * HEAD_DIM, pl.ds(start, n)]
        return jnp.concatenate([vh, jnp.ones((V7X_BF16_ROWS_PER_VREG, n), BF16)], axis=0)

    kd = k_ref[0, pl.ds(start, MOBA_BLOCK), :]
    key_pos = lax.broadcasted_iota(jnp.int32, (MOBA_BLOCK, tq), 0)
    qry_pos = lax.broadcasted_iota(jnp.int32, (MOBA_BLOCK, tq), 1)
    feat_pad = jnp.zeros((V7X_LANES - nb, tq), BF16)

    q_heads = [jnp.where((feat >= h * HEAD_DIM) & (feat < (h + 1) * HEAD_DIM), q_t, jnp.zeros_like(q_t))
               for h in range(HEADS_PER_PAIR)]
    qx = jnp.concatenate(
        [jnp.concatenate([q_heads[h], bias_ref[0, h * nb:(h + 1) * nb, pl.ds(start, tq)], feat_pad], axis=0)
         for h in range(HEADS_PER_PAIR)], axis=1)

    ck = KV_CHUNK_BLOCKS * MOBA_BLOCK
    n_chunks = (i + KV_CHUNK_BLOCKS - 1) // KV_CHUNK_BLOCKS

    def block_scores(c, t):
        ks = pl.multiple_of(c * ck + t * MOBA_BLOCK, MOBA_BLOCK)
        kx = jnp.concatenate([k_ref[0, pl.ds(ks, MOBA_BLOCK), :], blk_ref[pl.ds(ks, MOBA_BLOCK), :]], axis=1)
        return _dot(kx, qx)

    def run_step(c, cur, nxt, carry, n_cur=KV_CHUNK_BLOCKS, n_nxt=KV_CHUNK_BLOCKS):
        s_ref, mx_ref = cur
        m = [carry[2 * h] for h in range(HEADS_PER_PAIR)]
        acc = [carry[2 * h + 1] for h in range(HEADS_PER_PAIR)]
        if nxt is None:
            n_nxt = 0
        for t in range(max(n_cur, n_nxt)):
            rows = slice(t * MOBA_BLOCK, (t + 1) * MOBA_BLOCK)
            if t < n_nxt:
                s_nxt = block_scores(c + 1, t)
                nxt[0][rows, :] = s_nxt
                nxt[1][t:t + 1, :] = jnp.max(s_nxt, axis=0, keepdims=True)
            if t >= n_cur:
                continue
            ks = pl.multiple_of(c * ck + t * MOBA_BLOCK, MOBA_BLOCK)
            for h in range(HEADS_PER_PAIR):
                cols = slice(h * tq, (h + 1) * tq)
                m_new = jnp.maximum(m[h], mx_ref[t:t + 1, cols])
                p = jnp.exp(s_ref[rows, cols] - m_new).astype(BF16)
                acc[h] = jnp.exp(m[h] - m_new) * acc[h] + _dot(v_ext(h, ks, MOBA_BLOCK), p)
                m[h] = m_new
        return (m[0], acc[0], m[1], acc[1])

    even, odd = (s_even, mx_even), (s_odd, mx_odd)

    s_own = [jnp.where(key_pos <= qry_pos, _dot(kd, q_heads[h]), MASK_NEG) for h in range(HEADS_PER_PAIR)]
    for t in range(KV_CHUNK_BLOCKS):
        s_first = block_scores(0, t)
        s_even[t * MOBA_BLOCK:(t + 1) * MOBA_BLOCK, :] = s_first
        mx_even[t:t + 1, :] = jnp.max(s_first, axis=0, keepdims=True)
    carry = []
    for h in range(HEADS_PER_PAIR):
        m = jnp.max(s_own[h], axis=0, keepdims=True)
        p = jnp.exp(s_own[h] - m).astype(BF16)
        carry += [m, _dot(v_ext(h, start, MOBA_BLOCK), p)]

    def steps(first, count, carry):
        for j in range(count):
            carry = run_step(first + j, *((even, odd) if j % 2 == 0 else (odd, even)), carry)
        return carry

    prefetching = jnp.maximum(n_chunks - 1, 0)
    quads = prefetching // 4
    carry = lax.fori_loop(0, quads, lambda k, carry: steps(4 * k, 4, carry), tuple(carry))
    carry = lax.fori_loop(0, (prefetching - 4 * quads) // 2, lambda k, carry: steps(4 * quads, 2, carry), carry)

    last_blocks = i - (n_chunks - 1) * KV_CHUNK_BLOCKS

    def last_one(r):
        return lambda: finish(run_step(n_chunks - 1, even, None, carry, n_cur=r))

    def last_two(r):
        def run():
            mid = run_step(n_chunks - 2, even, odd, carry, n_nxt=r)
            return finish(run_step(n_chunks - 1, odd, None, mid, n_cur=r))
        return run

    counts = range(1, KV_CHUNK_BLOCKS + 1)

    def finish(carry):
        heads = []
        for h in range(HEADS_PER_PAIR):
            acc = carry[2 * h + 1]
            heads.append(acc[0:HEAD_DIM] / acc[HEAD_DIM:HEAD_DIM + 1])
        o_ref[0, pl.ds(start, tq), :] = jnp.concatenate(heads, axis=0).T
        return 0

    lax.cond(
        n_chunks % 2 == 1,
        lambda: lax.switch(last_blocks - 1, [last_one(r) for r in counts]),
        lambda: lax.cond(n_chunks > 0,
                         lambda: lax.switch(last_blocks - 1, [last_two(r) for r in counts]),
                         lambda: finish(carry)))


def _moba(q, k, v, bias):
    B, S, _ = k.shape
    nb = S // MOBA_BLOCK
    tq = MOBA_BLOCK
    ck = KV_CHUNK_BLOCKS * MOBA_BLOCK
    block_onehot = (jnp.arange(S)[:, None] // MOBA_BLOCK == jnp.arange(V7X_LANES)[None, :]).astype(BF16)
    feat_major = pl.BlockSpec((1, V7X_LANES, S), lambda b, p: (b, p, 0))
    return pl.pallas_call(
        _moba_kernel,
        grid=(B, N_PAIRS),
        in_specs=[
            feat_major,
            pl.BlockSpec((1, S, V7X_LANES), lambda b, p: (b, 0, p)),
            feat_major,
            pl.BlockSpec((1, HEADS_PER_PAIR * nb, S), lambda b, p: (b, p, 0)),
            pl.BlockSpec((S, V7X_LANES), lambda b, p: (0, 0), pipeline_mode=pl.Buffered(1)),
        ],
        out_specs=pl.BlockSpec((1, S, V7X_LANES), lambda b, p: (b, 0, p)),
        out_shape=jax.ShapeDtypeStruct((B, S, ATTN_WIDTH), F32),
        scratch_shapes=[
            pltpu.VMEM((ck, HEADS_PER_PAIR * tq), F32),
            pltpu.VMEM((ck, HEADS_PER_PAIR * tq), F32),
            pltpu.VMEM((V7X_SUBLANES, HEADS_PER_PAIR * tq), F32),
            pltpu.VMEM((V7X_SUBLANES, HEADS_PER_PAIR * tq), F32),
        ],
        compiler_params=pltpu.CompilerParams(
            dimension_semantics=("arbitrary", "arbitrary"),
            vmem_limit_bytes=V7X_VMEM_LIMIT_BYTES),
        name="moba",
    )(q, k, v, bias, block_onehot)


def _mixer_out_kernel(x_ref, lru_ref, attn_ref, mod_ref, ga_ref, w_out_ref, g2_ref, w_up_ref, cw_ref, cb_ref,
                      w_down_ref, o_ref, conv_carry, perm_in, perm_out):
    tm = x_ref.shape[1]

    @pl.when(pl.program_id(1) == 0)
    def _():
        conv_carry[...] = jnp.zeros_like(conv_carry)

    mod = mod_ref[0]
    gate1, sh2, sc2, gate2 = mod[2:3, :], mod[3:4, :], mod[4:5, :], mod[5:6, :]

    gain2 = g2_ref[...] * (1.0 + sc2)
    x1_parts, h2_parts = [], []
    for r0 in range(0, tm, tm // 2):
        rows = slice(r0, r0 + tm // 2)
        attn = attn_ref[0, rows, :]
        attn_n = (attn * _rms_scale(attn) * ga_ref[...]).astype(BF16)
        mixed = _dot(lru_ref[0, rows, :], w_out_ref[0:LRU_WIDTH, :]) + _dot(attn_n, w_out_ref[LRU_WIDTH:, :])
        x1_parts.append(x_ref[0, rows, :] + gate1 * mixed)
        h2_parts.append((x1_parts[-1] * _rms_scale(x1_parts[-1])) * gain2 + sh2)
    x1 = jnp.concatenate(x1_parts, axis=0)

    h2 = _to_run_major(perm_in, jnp.concatenate(h2_parts, axis=0)).astype(BF16)

    def up_conv(cols):
        return _causal_conv_run_major(_dot(h2, w_up_ref[:, cols]), cw_ref, cb_ref, conv_carry, cols)

    g = up_conv(slice(0, D_FF))
    val = up_conv(slice(D_FF, 2 * D_FF))
    act = (g * jax.nn.sigmoid(g) * val).astype(BF16)
    o_ref[0] = x1 + gate2 * _from_run_major(perm_out, _dot(act, w_down_ref[...]))


def _mixer_out(x, lru, attn, mod, ga, w_out, g2, w_up, cw, cb, w_down):
    B, S, _ = x.shape
    tm = TM_OUT
    const = lambda shape: pl.BlockSpec(shape, lambda b, s: (0,) * len(shape), pipeline_mode=pl.Buffered(1))
    tok = lambda w: pl.BlockSpec((1, tm, w), lambda b, s: (b, s, 0))
    return pl.pallas_call(
        _mixer_out_kernel,
        grid=(B, S // tm),
        in_specs=[
            tok(D_MODEL),
            tok(LRU_WIDTH),
            tok(ATTN_WIDTH),
            pl.BlockSpec((1, N_MOD, D_MODEL), lambda b, s: (b, 0, 0)),
            const((1, ATTN_WIDTH)),
            const((D_MODEL, D_MODEL)),
            const((1, D_MODEL)),
            const((D_MODEL, 2 * D_FF)),
            const((FFN_CONV, 2 * D_FF)),
            const((1, 2 * D_FF)),
            const((D_FF, D_MODEL)),
        ],
        out_specs=tok(D_MODEL),
        out_shape=jax.ShapeDtypeStruct((B, S, D_MODEL), F32),
        scratch_shapes=[
            pltpu.VMEM((V7X_SUBLANES, 2 * D_FF), F32),
            pltpu.VMEM((D_MODEL // V7X_LANES, tm, V7X_LANES), F32),
            pltpu.VMEM((D_MODEL // V7X_LANES, tm, V7X_LANES), F32),
        ],
        compiler_params=pltpu.CompilerParams(
            dimension_semantics=("arbitrary", "arbitrary"), vmem_limit_bytes=V7X_VMEM_LIMIT_BYTES),
        name="mixer_out",
    )(x, lru, attn, mod, ga, w_out, g2, w_up, cw, cb, w_down)


def _block_diag(w):
    n, d, _ = w.shape
    eye = jnp.eye(n, dtype=w.dtype)
    return jnp.einsum('hij,hg->higj', w, eye).reshape(n * d, n * d)


def _layer(x, c, w_ada, b_ada, norm1_g, w_in, q_norm_g, k_norm_g, lru_conv_w, lru_conv_b, lru_wa, lru_ba,
           lru_wx, lru_bx, lru_lambda, lru_out_g, attn_out_g, w_out, norm2_g, w_up, ffn_conv_w, ffn_conv_b,
           w_down):
    B, S, _ = x.shape
    row = lambda t: t.reshape(1, -1)
    mod = _modulation(c, w_ada, row(b_ada)).reshape(B, N_MOD, D_MODEL)

    head_id = jnp.arange(V7X_MXU_DIM) // HEAD_DIM
    gmat = jnp.where(head_id[:, None] == head_id[None, :], 1.0 / HEAD_DIM, 0.0).astype(BF16)
    heads_per_tile = V7X_MXU_DIM // (LRU_WIDTH // LRU_HEADS)
    w_gates = jnp.stack([
        jnp.concatenate([_block_diag(w[j * heads_per_tile:(j + 1) * heads_per_tile]) for w in (lru_wa, lru_wx)], axis=1)
        for j in range(LRU_HEADS // heads_per_tile)]).astype(BF16)
    b_gates = jnp.concatenate([lru_ba, lru_bx]).reshape(1, -1)

    q, k, v, bias, lru = _mixer_in(
        x, mod, row(norm1_g), w_in.astype(BF16), row(jnp.tile(q_norm_g, ATTN_HEADS)),
        row(jnp.tile(k_norm_g, ATTN_HEADS)), gmat, lru_conv_w, row(lru_conv_b), w_gates, b_gates,
        row(lru_lambda), row(lru_out_g))
    attn = _moba(q, k, v, bias)
    return _mixer_out(x, lru, attn, mod, row(attn_out_g), w_out.astype(BF16), row(norm2_g), w_up.astype(BF16),
                      ffn_conv_w, row(ffn_conv_b), w_down.astype(BF16))


def kernel(x, c, w_ada, b_ada, norm1_g, w_in, q_norm_g, k_norm_g, lru_conv_w, lru_conv_b, lru_wa, lru_ba, lru_wx, lru_bx, lru_lambda, lru_out_g, attn_out_g, w_out, norm2_g, w_up, ffn_conv_w, ffn_conv_b, w_down):
    depth = w_ada.shape[0]
    for l in range(depth):
        x = _layer(x, c, w_ada[l], b_ada[l], norm1_g[l], w_in[l], q_norm_g[l], k_norm_g[l], lru_conv_w[l],
                   lru_conv_b[l], lru_wa[l], lru_ba[l], lru_wx[l], lru_bx[l], lru_lambda[l], lru_out_g[l],
                   attn_out_g[l], w_out[l], norm2_g[l], w_up[l], ffn_conv_w[l], ffn_conv_b[l], w_down[l])
    return x
```

```python
import functools
import math

import jax
import jax.numpy as jnp
from jax import lax
from jax.experimental import pallas as pl
from jax.experimental.pallas import tpu as pltpu

D_MODEL = 1024
ATTN_HEADS = 8
HEAD_DIM = 64
ATTN_WIDTH = ATTN_HEADS * HEAD_DIM
MOBA_BLOCK = 256
MOBA_TOPK = 3
LRU_WIDTH = D_MODEL - ATTN_WIDTH
LRU_HEADS = 8
LRU_CONV = 4
LRU_C = 8.0
N_IN = 3 * ATTN_WIDTH + 2 * LRU_WIDTH
D_FF = 2816
FFN_CONV = 3
N_MOD = 6
EPS = 1e-6

V7X_LANES = 128
V7X_SUBLANES = 8
V7X_MXU_DIM = 256
V7X_BF16_ROWS_PER_VREG = 16
V7X_VMEM_LIMIT_BYTES = 56 * 1024 * 1024

HEADS_PER_PAIR = V7X_LANES // HEAD_DIM
N_PAIRS = ATTN_HEADS // HEADS_PER_PAIR
MASK_NEG = -1e30

TM_IN = 1024
TM_OUT = 512
KV_CHUNK_BLOCKS = 4
SCORE_BUFFER_SKEW_ROWS = 8
MOD_CHUNK = 1536

BF16 = jnp.bfloat16
F32 = jnp.float32


def _dot(a, b):
    return jnp.dot(a, b, preferred_element_type=F32)


def _rms_scale(y):
    return lax.rsqrt(jnp.mean(y * y, axis=-1, keepdims=True) + EPS)


def _mod_kernel(c_ref, w_ref, b_ref, o_ref):
    o_ref[...] = _dot(c_ref[...], w_ref[...]) + b_ref[...]


def _modulation(c, w_ada, b_ada):
    B = c.shape[0]
    n = w_ada.shape[1]
    return pl.pallas_call(
        _mod_kernel,
        grid=(n // MOD_CHUNK,),
        in_specs=[
            pl.BlockSpec((B, D_MODEL), lambda j: (0, 0)),
            pl.BlockSpec((D_MODEL, MOD_CHUNK), lambda j: (0, j)),
            pl.BlockSpec((1, MOD_CHUNK), lambda j: (0, j)),
        ],
        out_specs=pl.BlockSpec((B, MOD_CHUNK), lambda j: (0, j)),
        out_shape=jax.ShapeDtypeStruct((B, n), F32),
        compiler_params=pltpu.CompilerParams(
            dimension_semantics=("arbitrary",), vmem_limit_bytes=V7X_VMEM_LIMIT_BYTES),
        name="modulation",
    )(c, w_ada, b_ada)


def _head_rmsnorm(t, gmat, gain):
    sq = t * t
    hi = sq.astype(BF16)
    lo = (sq - hi.astype(F32)).astype(BF16)
    ms = jnp.concatenate(
        [_dot(hi[:, c:c + V7X_MXU_DIM], gmat) + _dot(lo[:, c:c + V7X_MXU_DIM], gmat)
         for c in range(0, t.shape[1], V7X_MXU_DIM)], axis=1)
    return t * lax.rsqrt(ms + EPS) * gain


def _shift_rows(t, d, fill):
    row = lax.broadcasted_iota(jnp.int32, t.shape, 0)
    return jnp.where(row >= d, pltpu.roll(t, d, 0), fill)


SCAN_RUN = V7X_SUBLANES
SCAN_GROUP = SCAN_RUN * V7X_SUBLANES


def _to_run_major(buf, t):
    tm, width = t.shape
    lane_groups = width // V7X_LANES
    for l in range(lane_groups):
        buf[l] = t[:, l * V7X_LANES:(l + 1) * V7X_LANES]
    pieces = []
    for g in range(tm // SCAN_GROUP):
        for s in range(SCAN_RUN):
            rows = pl.ds(g * SCAN_GROUP + s, V7X_SUBLANES, stride=SCAN_RUN)
            pieces.append(jnp.concatenate([buf[l, rows, :] for l in range(lane_groups)], axis=1))
    return jnp.concatenate(pieces, axis=0)


def _from_run_major(buf, t):
    tm, width = t.shape
    lane_groups = width // V7X_LANES
    for g in range(tm // SCAN_GROUP):
        for s in range(SCAN_RUN):
            piece = t[g * SCAN_GROUP + s * V7X_SUBLANES:g * SCAN_GROUP + (s + 1) * V7X_SUBLANES]
            rows = pl.ds(g * SCAN_GROUP + s, V7X_SUBLANES, stride=SCAN_RUN)
            for l in range(lane_groups):
                buf[l, rows, :] = piece[:, l * V7X_LANES:(l + 1) * V7X_LANES]
    return jnp.concatenate([buf[l] for l in range(lane_groups)], axis=1)


def _causal_conv_run_major(xp, w_ref, b_ref, carry_ref, cols=slice(None)):
    taps = w_ref.shape[0]
    tm, width = xp.shape
    row = lax.broadcasted_iota(jnp.int32, (V7X_SUBLANES, width), 0)
    wrapped = range(SCAN_RUN - (taps - 1), SCAN_RUN)
    above = {s: carry_ref[s:s + 1, cols] for s in wrapped}
    bias = b_ref[:, cols]
    w = [w_ref[k:k + 1, cols] for k in range(taps)]
    out = []
    for g in range(tm // SCAN_GROUP):
        def piece(s):
            return xp[g * SCAN_GROUP + s * V7X_SUBLANES:g * SCAN_GROUP + (s + 1) * V7X_SUBLANES]

        prev_run = {}
        for s in wrapped:
            prev_run[s] = jnp.where(row >= 1, pltpu.roll(piece(s), 1, 0), above[s])
            above[s] = piece(s)[V7X_SUBLANES - 1:V7X_SUBLANES]
        for s in range(SCAN_RUN):
            acc = bias + w[taps - 1] * piece(s)
            for d in range(1, taps):
                src = piece(s - d) if s >= d else prev_run[s - d + SCAN_RUN]
                acc = acc + w[taps - 1 - d] * src
            out.append(acc)
    for s in wrapped:
        carry_ref[s:s + 1, cols] = above[s]
    return jnp.concatenate(out, axis=0)


def _linear_scan_run_major(a, u, h_in):
    tm = a.shape[0]
    row = lax.broadcasted_iota(jnp.int32, (V7X_SUBLANES, a.shape[1]), 0)
    out = []
    for g in range(tm // SCAN_GROUP):
        h_loc, a_run = [], []
        for s in range(SCAN_RUN):
            rows = slice(g * SCAN_GROUP + s * V7X_SUBLANES, g * SCAN_GROUP + (s + 1) * V7X_SUBLANES)
            h_loc.append(u[rows] if s == 0 else a[rows] * h_loc[-1] + u[rows])
            a_run.append(a[rows] if s == 0 else a[rows] * a_run[-1])
        h_end, a_end = h_loc[-1], a_run[-1]
        d = 1
        while d < V7X_SUBLANES:
            h_end = a_end * _shift_rows(h_end, d, 0.0) + h_end
            a_end = a_end * _shift_rows(a_end, d, 1.0)
            d *= 2
        after = h_end + a_end * h_in
        before = jnp.where(row >= 1, pltpu.roll(after, 1, 0), h_in)
        h_in = after[V7X_SUBLANES - 1:V7X_SUBLANES]
        out += [h_loc[s] + a_run[s] * before for s in range(SCAN_RUN)]
    return jnp.concatenate(out, axis=0), h_in


def _gelu_tanh(t):
    return 0.5 * t * (1.0 + jnp.tanh(math.sqrt(2.0 / math.pi) * (t + 0.044715 * (t * t * t))))


def _mixer_in_kernel(x_ref, mod_ref, g1_ref, w_in_ref, gq_ref, gk_ref, gmat_ref, cw_ref, cb_ref,
                     wg_ref, bg_ref, lam_ref, go_ref,
                     q_ref, k_ref, v_ref, bias_ref, lru_ref,
                     conv_carry, h_carry, km_all, scan_buf):
    @pl.when(pl.program_id(1) == 0)
    def _():
        conv_carry[...] = jnp.zeros_like(conv_carry)
        h_carry[...] = jnp.zeros_like(h_carry)
        km_all[...] = jnp.zeros_like(km_all)

    blocks_per_step = x_ref.shape[1] // MOBA_BLOCK
    mod = mod_ref[0]
    gain = g1_ref[...] * (1.0 + mod[1:2, :])
    shift = mod[0:1, :]
    normed = []
    for sub in range(blocks_per_step):
        x = x_ref[0, sub * MOBA_BLOCK:(sub + 1) * MOBA_BLOCK, :]
        normed.append(((x * _rms_scale(x)) * gain + shift).astype(BF16))
    for sub in range(blocks_per_step):
        _attention_side(sub, pl.program_id(1) * blocks_per_step + sub, normed[sub], w_in_ref, gq_ref, gk_ref,
                        gmat_ref, q_ref, k_ref, v_ref, bias_ref, km_all)
    for sub in range(blocks_per_step):
        _lru_side(sub, normed[sub], w_in_ref, cw_ref, cb_ref, wg_ref, bg_ref, lam_ref, go_ref, lru_ref,
                  conv_carry, h_carry, scan_buf)


def _in_proj(hb, w_in_ref, j):
    return _dot(hb, w_in_ref[:, j * ATTN_WIDTH:(j + 1) * ATTN_WIDTH])


def _attention_side(sub, blk_id, hb, w_in_ref, gq_ref, gk_ref, gmat_ref, q_ref, k_ref, v_ref, bias_ref, km_all):
    tm = MOBA_BLOCK
    rows = slice(sub * tm, (sub + 1) * tm)
    proj = functools.partial(_in_proj, hb, w_in_ref)
    gmat = gmat_ref[...]

    kn = _head_rmsnorm(proj(1), gmat, gk_ref[...])
    k_ref[0, rows, :] = kn.astype(BF16)
    km_all[pl.ds(blk_id, 1), :] = jnp.mean(kn, axis=0, keepdims=True)

    qn =_head_rmsnorm(proj(0), gmat, gq_ref[...] * (HEAD_DIM ** -0.5))
    q_t = qn.T.astype(BF16)
    q_ref[0, :, rows] = q_t

    nb = km_all.shape[0]
    km = km_all[...].astype(BF16)
    feat = lax.broadcasted_iota(jnp.int32, (V7X_LANES, 1), 0)
    for head in range(ATTN_HEADS):
        p, h = divmod(head, HEADS_PER_PAIR)
        q_pair = q_t[p * V7X_LANES:(p + 1) * V7X_LANES]
        qh = jnp.where((feat >= h * HEAD_DIM) & (feat < (h + 1) * HEAD_DIM), q_pair, jnp.zeros_like(q_pair))
        gate = _dot(km[:, p * V7X_LANES:(p + 1) * V7X_LANES], qh)
        bias_ref[0, head * nb:(head + 1) * nb, rows] = _select_bias(gate, blk_id).astype(BF16)

    v_ref[0, :, rows] = proj(2).T.astype(BF16)


def _lru_side(sub, hb, w_in_ref, cw_ref, cb_ref, wg_ref, bg_ref, lam_ref, go_ref, lru_ref, conv_carry, h_carry,
              scan_buf):
    tm = MOBA_BLOCK
    rows = slice(sub * tm, (sub + 1) * tm)
    proj = functools.partial(_in_proj, hb, w_in_ref)

    xc = _causal_conv_run_major(_to_run_major(scan_buf.at[2 * sub], proj(3)), cw_ref, cb_ref, conv_carry)
    xcb = xc.astype(BF16)
    pre = [_dot(xcb[:, j * V7X_MXU_DIM:(j + 1) * V7X_MXU_DIM], wg_ref[j]) for j in range(LRU_WIDTH // V7X_MXU_DIM)]
    bg = bg_ref[...]
    r = jax.nn.sigmoid(jnp.concatenate([g[:, :V7X_MXU_DIM] for g in pre], axis=1) + bg[:, :LRU_WIDTH])
    i = jax.nn.sigmoid(jnp.concatenate([g[:, V7X_MXU_DIM:] for g in pre], axis=1) + bg[:, LRU_WIDTH:])
    neg_lam = -lam_ref[...]
    softplus = jnp.maximum(neg_lam, 0.0) + jnp.log1p(jnp.exp(-jnp.abs(neg_lam)))
    log_a = r * ((-LRU_C) * softplus)
    a = jnp.exp(log_a)
    th = jnp.tanh(log_a)
    u = jnp.sqrt(-2.0 * th / (1.0 - th)) * (i * xc)
    hs, h_last = _linear_scan_run_major(a, u, h_carry[0:1, :])
    h_carry[0:1, :] = h_last
    hs = _from_run_major(scan_buf.at[2 * sub + 1], hs)

    y = hs * _gelu_tanh(proj(4))
    lru_ref[0, rows, :] = (y * _rms_scale(y) * go_ref[...]).astype(BF16)


def _mixer_in(x, mod, g1, w_in, gq, gk, gmat, cw, cb, wg, bg, lam, go):
    B, S, _ = x.shape
    tm = TM_IN
    assert tm % MOBA_BLOCK == 0, "a grid step covers whole MoBA blocks: the block choice reads the means of earlier blocks"
    nb = S // MOBA_BLOCK
    const = lambda shape: pl.BlockSpec(shape, lambda b, s: (0,) * len(shape))
    tok = lambda w: pl.BlockSpec((1, tm, w), lambda b, s: (b, s, 0))
    feat_major = pl.BlockSpec((1, ATTN_WIDTH, tm), lambda b, s: (b, 0, s))
    return pl.pallas_call(
        _mixer_in_kernel,
        grid=(B, S // tm),
        in_specs=[
            tok(D_MODEL),
            pl.BlockSpec((1, N_MOD, D_MODEL), lambda b, s: (b, 0, 0)),
            const((1, D_MODEL)),
            const((D_MODEL, N_IN)),
            const((1, ATTN_WIDTH)),
            const((1, ATTN_WIDTH)),
            const((V7X_MXU_DIM, V7X_MXU_DIM)),
            const((LRU_CONV, LRU_WIDTH)),
            const((1, LRU_WIDTH)),
            const((LRU_WIDTH // V7X_MXU_DIM, V7X_MXU_DIM, 2 * V7X_MXU_DIM)),
            const((1, 2 * LRU_WIDTH)),
            const((1, LRU_WIDTH)),
            const((1, LRU_WIDTH)),
        ],
        out_specs=[
            feat_major,
            tok(ATTN_WIDTH),
            feat_major,
            pl.BlockSpec((1, ATTN_HEADS * nb, tm), lambda b, s: (b, 0, s)),
            tok(LRU_WIDTH),
        ],
        out_shape=[
            jax.ShapeDtypeStruct((B, ATTN_WIDTH, S), BF16),
            jax.ShapeDtypeStruct((B, S, ATTN_WIDTH), BF16),
            jax.ShapeDtypeStruct((B, ATTN_WIDTH, S), BF16),
            jax.ShapeDtypeStruct((B, ATTN_HEADS * nb, S), BF16),
            jax.ShapeDtypeStruct((B, S, LRU_WIDTH), BF16),
        ],
        scratch_shapes=[
            pltpu.VMEM((V7X_SUBLANES, LRU_WIDTH), F32),
            pltpu.VMEM((V7X_SUBLANES, LRU_WIDTH), F32),
            pltpu.VMEM((nb, ATTN_WIDTH), F32),
            pltpu.VMEM((2 * (tm // MOBA_BLOCK), LRU_WIDTH // V7X_LANES, MOBA_BLOCK, V7X_LANES), F32),
        ],
        compiler_params=pltpu.CompilerParams(
            dimension_semantics=("arbitrary", "arbitrary"), vmem_limit_bytes=V7X_VMEM_LIMIT_BYTES),
        name="mixer_in",
    )(x, mod, g1, w_in, gq, gk, gmat, cw, cb, wg, bg, lam, go)


def _select_bias(gate, n_past):
    nb = gate.shape[0]
    blk = lax.broadcasted_iota(jnp.int32, gate.shape, 0)
    neg_inf = jnp.float32(-jnp.inf)
    g = jnp.where(blk < n_past, gate, neg_inf)
    bias = jnp.full(gate.shape, MASK_NEG, F32)
    for _ in range(MOBA_TOPK):
        mx = jnp.max(g, axis=0, keepdims=True)
        cand = jnp.where((g == mx) & (mx > neg_inf), blk, nb)
        pick = blk == jnp.min(cand, axis=0, keepdims=True)
        bias = jnp.where(pick, 0.0, bias)
        g = jnp.where(pick, neg_inf, g)
    return bias


def _moba_kernel(q_ref, k_ref, v_ref, bias_ref, blk_ref, o_ref, s_even, s_odd, mx_even, mx_odd):
    def tile(i, _):
        _moba_tile(i, q_ref, k_ref, v_ref, bias_ref, blk_ref, o_ref, s_even, s_odd, mx_even, mx_odd)
        return 0

    lax.fori_loop(0, q_ref.shape[2] // MOBA_BLOCK, tile, 0)


def _moba_tile(i, q_ref, k_ref, v_ref, bias_ref, blk_ref, o_ref, s_even, s_odd, mx_even, mx_odd):
    tq = MOBA_BLOCK
    nb = bias_ref.shape[1] // HEADS_PER_PAIR
    feat = lax.broadcasted_iota(jnp.int32, (V7X_LANES, 1), 0)
    start = pl.multiple_of(i * tq, tq)
    q_t = q_ref[0, :, pl.ds(start, tq)]

    def v_ext(h, start, n):
        vh = v_ref[0, h * HEAD_DIM:(h + 1) * HEAD_DIM, pl.ds(start, n)]
        return jnp.concatenate([vh, jnp.ones((V7X_BF16_ROWS_PER_VREG, n), BF16)], axis=0)

    kd = k_ref[0, pl.ds(start, MOBA_BLOCK), :]
    key_pos = lax.broadcasted_iota(jnp.int32, (MOBA_BLOCK, tq), 0)
    qry_pos = lax.broadcasted_iota(jnp.int32, (MOBA_BLOCK, tq), 1)
    feat_pad = jnp.zeros((V7X_LANES - nb, tq), BF16)

    q_heads = [jnp.where((feat >= h * HEAD_DIM) & (feat < (h + 1) * HEAD_DIM), q_t, jnp.zeros_like(q_t))
               for h in range(HEADS_PER_PAIR)]
    qx = jnp.concatenate(
        [jnp.concatenate([q_heads[h], bias_ref[0, h * nb:(h + 1) * nb, pl.ds(start, tq)], feat_pad], axis=0)
         for h in range(HEADS_PER_PAIR)], axis=1)

    ck = KV_CHUNK_BLOCKS * MOBA_BLOCK
    n_chunks = (i + KV_CHUNK_BLOCKS - 1) // KV_CHUNK_BLOCKS

    def block_scores(c, t):
        ks = pl.multiple_of(c * ck + t * MOBA_BLOCK, MOBA_BLOCK)
        kx = jnp.concatenate([k_ref[0, pl.ds(ks, MOBA_BLOCK), :], blk_ref[pl.ds(ks, MOBA_BLOCK), :]], axis=1)
        return _dot(kx, qx)

    def run_step(c, cur, nxt, carry, n_cur=KV_CHUNK_BLOCKS, n_nxt=KV_CHUNK_BLOCKS):
        s_ref, mx_ref = cur
        m = [carry[2 * h] for h in range(HEADS_PER_PAIR)]
        acc = [carry[2 * h + 1] for h in range(HEADS_PER_PAIR)]
        if nxt is None:
            n_nxt = 0
        for t in range(max(n_cur, n_nxt)):
            rows = slice(t * MOBA_BLOCK, (t + 1) * MOBA_BLOCK)
            if t < n_nxt:
                s_nxt = block_scores(c + 1, t)
                nxt[0][rows, :] = s_nxt
                nxt[1][t:t + 1, :] = jnp.max(s_nxt, axis=0, keepdims=True)
            if t >= n_cur:
                continue
            ks = pl.multiple_of(c * ck + t * MOBA_BLOCK, MOBA_BLOCK)
            for h in range(HEADS_PER_PAIR):
                cols = slice(h * tq, (h + 1) * tq)
                m_new = jnp.maximum(m[h], mx_ref[t:t + 1, cols])
                p = jnp.exp(s_ref[rows, cols] - m_new).astype(BF16)
                acc[h] = jnp.exp(m[h] - m_new) * acc[h] + _dot(v_ext(h, ks, MOBA_BLOCK), p)
                m[h] = m_new
        return (m[0], acc[0], m[1], acc[1])

    even, odd = (s_even, mx_even), (s_odd.at[pl.ds(SCORE_BUFFER_SKEW_ROWS, KV_CHUNK_BLOCKS * MOBA_BLOCK)], mx_odd)

    s_own = [jnp.where(key_pos <= qry_pos, _dot(kd, q_heads[h]), MASK_NEG) for h in range(HEADS_PER_PAIR)]
    for t in range(KV_CHUNK_BLOCKS):
        s_first = block_scores(0, t)
        s_even[t * MOBA_BLOCK:(t + 1) * MOBA_BLOCK, :] = s_first
        mx_even[t:t + 1, :] = jnp.max(s_first, axis=0, keepdims=True)
    carry = []
    for h in range(HEADS_PER_PAIR):
        m = jnp.max(s_own[h], axis=0, keepdims=True)
        p = jnp.exp(s_own[h] - m).astype(BF16)
        carry += [m, _dot(v_ext(h, start, MOBA_BLOCK), p)]

    def steps(first, count, carry):
        for j in range(count):
            carry = run_step(first + j, *((even, odd) if j % 2 == 0 else (odd, even)), carry)
        return carry

    prefetching = jnp.maximum(n_chunks - 1, 0)
    quads = prefetching // 4
    carry = lax.fori_loop(0, quads, lambda k, carry: steps(4 * k, 4, carry), tuple(carry))
    carry = lax.fori_loop(0, (prefetching - 4 * quads) // 2, lambda k, carry: steps(4 * quads, 2, carry), carry)

    last_blocks = i - (n_chunks - 1) * KV_CHUNK_BLOCKS

    def last_one(r):
        return lambda: finish(run_step(n_chunks - 1, even, None, carry, n_cur=r))

    def last_two(r):
        def run():
            mid = run_step(n_chunks - 2, even, odd, carry, n_nxt=r)
            return finish(run_step(n_chunks - 1, odd, None, mid, n_cur=r))
        return run

    counts = range(1, KV_CHUNK_BLOCKS + 1)

    def finish(carry):
        heads = []
        for h in range(HEADS_PER_PAIR):
            acc = carry[2 * h + 1]
            heads.append(acc[0:HEAD_DIM] / acc[HEAD_DIM:HEAD_DIM + 1])
        o_ref[0, pl.ds(start, tq), :] = jnp.concatenate(heads, axis=0).T
        return 0

    lax.cond(
        n_chunks % 2 == 1,
        lambda: lax.switch(last_blocks - 1, [last_one(r) for r in counts]),
        lambda: lax.cond(n_chunks > 0,
                         lambda: lax.switch(last_blocks - 1, [last_two(r) for r in counts]),
                         lambda: finish(carry)))


def _moba(q, k, v, bias):
    B, S, _ = k.shape
    nb = S // MOBA_BLOCK
    tq = MOBA_BLOCK
    ck = KV_CHUNK_BLOCKS * MOBA_BLOCK
    block_onehot = (jnp.arange(S)[:, None] // MOBA_BLOCK == jnp.arange(V7X_LANES)[None, :]).astype(BF16)
    feat_major = pl.BlockSpec((1, V7X_LANES, S), lambda b, p: (b, p, 0))
    return pl.pallas_call(
        _moba_kernel,
        grid=(B, N_PAIRS),
        in_specs=[
            feat_major,
            pl.BlockSpec((1, S, V7X_LANES), lambda b, p: (b, 0, p)),
            feat_major,
            pl.BlockSpec((1, HEADS_PER_PAIR * nb, S), lambda b, p: (b, p, 0)),
            pl.BlockSpec((S, V7X_LANES), lambda b, p: (0, 0), pipeline_mode=pl.Buffered(1)),
        ],
        out_specs=pl.BlockSpec((1, S, V7X_LANES), lambda b, p: (b, 0, p)),
        out_shape=jax.ShapeDtypeStruct((B, S, ATTN_WIDTH), F32),
        scratch_shapes=[
            pltpu.VMEM((ck, HEADS_PER_PAIR * tq), F32),
            pltpu.VMEM((ck + SCORE_BUFFER_SKEW_ROWS, HEADS_PER_PAIR * tq), F32),
            pltpu.VMEM((V7X_SUBLANES, HEADS_PER_PAIR * tq), F32),
            pltpu.VMEM((V7X_SUBLANES, HEADS_PER_PAIR * tq), F32),
        ],
        compiler_params=pltpu.CompilerParams(
            dimension_semantics=("arbitrary", "arbitrary"),
            vmem_limit_bytes=V7X_VMEM_LIMIT_BYTES),
        name="moba",
    )(q, k, v, bias, block_onehot)


def _mixer_out_kernel(x_ref, lru_ref, attn_ref, mod_ref, ga_ref, w_out_ref, g2_ref, w_up_ref, cw_ref, cb_ref,
                      w_down_ref, o_ref, conv_carry, perm_in, perm_out):
    tm = x_ref.shape[1]

    @pl.when(pl.program_id(1) == 0)
    def _():
        conv_carry[...] = jnp.zeros_like(conv_carry)

    mod = mod_ref[0]
    gate1, sh2, sc2, gate2 = mod[2:3, :], mod[3:4, :], mod[4:5, :], mod[5:6, :]

    gain2 = g2_ref[...] * (1.0 + sc2)
    x1_parts, h2_parts = [], []
    for r0 in range(0, tm, tm // 2):
        rows = slice(r0, r0 + tm // 2)
        attn = attn_ref[0, rows, :]
        attn_n = (attn * _rms_scale(attn) * ga_ref[...]).astype(BF16)
        mixed = _dot(lru_ref[0, rows, :], w_out_ref[0:LRU_WIDTH, :]) + _dot(attn_n, w_out_ref[LRU_WIDTH:, :])
        x1_parts.append(x_ref[0, rows, :] + gate1 * mixed)
        h2_parts.append((x1_parts[-1] * _rms_scale(x1_parts[-1])) * gain2 + sh2)
    x1 = jnp.concatenate(x1_parts, axis=0)

    h2 = _to_run_major(perm_in, jnp.concatenate(h2_parts, axis=0)).astype(BF16)

    def up_conv(cols):
        return _causal_conv_run_major(_dot(h2, w_up_ref[:, cols]), cw_ref, cb_ref, conv_carry, cols)

    g = up_conv(slice(0, D_FF))
    val = up_conv(slice(D_FF, 2 * D_FF))
    act = (g * jax.nn.sigmoid(g) * val).astype(BF16)
    o_ref[0] = x1 + gate2 * _from_run_major(perm_out, _dot(act, w_down_ref[...]))


def _mixer_out(x, lru, attn, mod, ga, w_out, g2, w_up, cw, cb, w_down):
    B, S, _ = x.shape
    tm = TM_OUT
    const = lambda shape: pl.BlockSpec(shape, lambda b, s: (0,) * len(shape), pipeline_mode=pl.Buffered(1))
    tok = lambda w: pl.BlockSpec((1, tm, w), lambda b, s: (b, s, 0))
    return pl.pallas_call(
        _mixer_out_kernel,
        grid=(B, S // tm),
        in_specs=[
            tok(D_MODEL),
            tok(LRU_WIDTH),
            tok(ATTN_WIDTH),
            pl.BlockSpec((1, N_MOD, D_MODEL), lambda b, s: (b, 0, 0)),
            const((1, ATTN_WIDTH)),
            const((D_MODEL, D_MODEL)),
            const((1, D_MODEL)),
            const((D_MODEL, 2 * D_FF)),
            const((FFN_CONV, 2 * D_FF)),
            const((1, 2 * D_FF)),
            const((D_FF, D_MODEL)),
        ],
        out_specs=tok(D_MODEL),
        out_shape=jax.ShapeDtypeStruct((B, S, D_MODEL), F32),
        scratch_shapes=[
            pltpu.VMEM((V7X_SUBLANES, 2 * D_FF), F32),
            pltpu.VMEM((D_MODEL // V7X_LANES, tm, V7X_LANES), F32),
            pltpu.VMEM((D_MODEL // V7X_LANES, tm, V7X_LANES), F32),
        ],
        compiler_params=pltpu.CompilerParams(
            dimension_semantics=("arbitrary", "arbitrary"), vmem_limit_bytes=V7X_VMEM_LIMIT_BYTES),
        name="mixer_out",
    )(x, lru, attn, mod, ga, w_out, g2, w_up, cw, cb, w_down)


def _block_diag(w):
    n, d, _ = w.shape
    eye = jnp.eye(n, dtype=w.dtype)
    return jnp.einsum('hij,hg->higj', w, eye).reshape(n * d, n * d)


def _layer(x, c, w_ada, b_ada, norm1_g, w_in, q_norm_g, k_norm_g, lru_conv_w, lru_conv_b, lru_wa, lru_ba,
           lru_wx, lru_bx, lru_lambda, lru_out_g, attn_out_g, w_out, norm2_g, w_up, ffn_conv_w, ffn_conv_b,
           w_down):
    B, S, _ = x.shape
    row = lambda t: t.reshape(1, -1)
    mod = _modulation(c, w_ada, row(b_ada)).reshape(B, N_MOD, D_MODEL)

    head_id = jnp.arange(V7X_MXU_DIM) // HEAD_DIM
    gmat = jnp.where(head_id[:, None] == head_id[None, :], 1.0 / HEAD_DIM, 0.0).astype(BF16)
    heads_per_tile = V7X_MXU_DIM // (LRU_WIDTH // LRU_HEADS)
    w_gates = jnp.stack([
        jnp.concatenate([_block_diag(w[j * heads_per_tile:(j + 1) * heads_per_tile]) for w in (lru_wa, lru_wx)], axis=1)
        for j in range(LRU_HEADS // heads_per_tile)]).astype(BF16)
    b_gates = jnp.concatenate([lru_ba, lru_bx]).reshape(1, -1)

    q, k, v, bias, lru = _mixer_in(
        x, mod, row(norm1_g), w_in.astype(BF16), row(jnp.tile(q_norm_g, ATTN_HEADS)),
        row(jnp.tile(k_norm_g, ATTN_HEADS)), gmat, lru_conv_w, row(lru_conv_b), w_gates, b_gates,
        row(lru_lambda), row(lru_out_g))
    attn = _moba(q, k, v, bias)
    return _mixer_out(x, lru, attn, mod, row(attn_out_g), w_out.astype(BF16), row(norm2_g), w_up.astype(BF16),
                      ffn_conv_w, row(ffn_conv_b), w_down.astype(BF16))


def kernel(x, c, w_ada, b_ada, norm1_g, w_in, q_norm_g, k_norm_g, lru_conv_w, lru_conv_b, lru_wa, lru_ba, lru_wx, lru_bx, lru_lambda, lru_out_g, attn_out_g, w_out, norm2_g, w_up, ffn_conv_w, ffn_conv_b, w_down):
    depth = w_ada.shape[0]
    for l in range(depth):
        x = _layer(x, c, w_ada[l], b_ada[l], norm1_g[l], w_in[l], q_norm_g[l], k_norm_g[l], lru_conv_w[l],
                   lru_conv_b[l], lru_wa[l], lru_ba[l], lru_wx[l], lru_bx[l], lru_lambda[l], lru_out_g[l],
                   attn_out_g[l], w_out[l], norm2_g[l], w_up[l], ffn_conv_w[l], ffn_conv_b[l], w_down[l])
    return x
```

```python
import functools
import math

import jax
import jax.numpy as jnp
from jax import lax
from jax.experimental import pallas as pl
from jax.experimental.pallas import tpu as pltpu

D_MODEL = 1024
ATTN_HEADS = 8
HEAD_DIM = 64
ATTN_WIDTH = ATTN_HEADS * HEAD_DIM
MOBA_BLOCK = 256
MOBA_TOPK = 3
LRU_WIDTH = D_MODEL - ATTN_WIDTH
LRU_HEADS = 8
LRU_CONV = 4
LRU_C = 8.0
N_IN = 3 * ATTN_WIDTH + 2 * LRU_WIDTH
D_FF = 2816
FFN_CONV = 3
N_MOD = 6
EPS = 1e-6

V7X_LANES = 128
V7X_SUBLANES = 8
V7X_MXU_DIM = 256
V7X_BF16_ROWS_PER_VREG = 16
V7X_VMEM_LIMIT_BYTES = 56 * 1024 * 1024

HEADS_PER_PAIR = V7X_LANES // HEAD_DIM
N_PAIRS = ATTN_HEADS // HEADS_PER_PAIR
MASK_NEG = -1e30

TM_IN = 1024
TM_OUT = 512
KV_CHUNK_BLOCKS = 4
MOD_CHUNK = 1536

BF16 = jnp.bfloat16
F32 = jnp.float32


def _dot(a, b):
    return jnp.dot(a, b, preferred_element_type=F32)


def _rms_scale(y):
    return lax.rsqrt(jnp.mean(y * y, axis=-1, keepdims=True) + EPS)


def _mod_kernel(c_ref, w_ref, b_ref, o_ref):
    o_ref[...] = _dot(c_ref[...], w_ref[...]) + b_ref[...]


def _modulation(c, w_ada, b_ada):
    B = c.shape[0]
    n = w_ada.shape[1]
    return pl.pallas_call(
        _mod_kernel,
        grid=(n // MOD_CHUNK,),
        in_specs=[
            pl.BlockSpec((B, D_MODEL), lambda j: (0, 0)),
            pl.BlockSpec((D_MODEL, MOD_CHUNK), lambda j: (0, j)),
            pl.BlockSpec((1, MOD_CHUNK), lambda j: (0, j)),
        ],
        out_specs=pl.BlockSpec((B, MOD_CHUNK), lambda j: (0, j)),
        out_shape=jax.ShapeDtypeStruct((B, n), F32),
        compiler_params=pltpu.CompilerParams(
            dimension_semantics=("arbitrary",), vmem_limit_bytes=V7X_VMEM_LIMIT_BYTES),
        name="modulation",
    )(c, w_ada, b_ada)


def _head_rmsnorm(t, gmat, gain):
    sq = t * t
    hi = sq.astype(BF16)
    lo = (sq - hi.astype(F32)).astype(BF16)
    ms = jnp.concatenate(
        [_dot(hi[:, c:c + V7X_MXU_DIM], gmat) + _dot(lo[:, c:c + V7X_MXU_DIM], gmat)
         for c in range(0, t.shape[1], V7X_MXU_DIM)], axis=1)
    return t * lax.rsqrt(ms + EPS) * gain


def _shift_rows(t, d, fill):
    row = lax.broadcasted_iota(jnp.int32, t.shape, 0)
    return jnp.where(row >= d, pltpu.roll(t, d, 0), fill)


SCAN_RUN = V7X_SUBLANES
SCAN_GROUP = SCAN_RUN * V7X_SUBLANES


def _to_run_major(buf, t):
    tm, width = t.shape
    lane_groups = width // V7X_LANES
    for l in range(lane_groups):
        buf[l] = t[:, l * V7X_LANES:(l + 1) * V7X_LANES]
    pieces = []
    for g in range(tm // SCAN_GROUP):
        for s in range(SCAN_RUN):
            rows = pl.ds(g * SCAN_GROUP + s, V7X_SUBLANES, stride=SCAN_RUN)
            pieces.append(jnp.concatenate([buf[l, rows, :] for l in range(lane_groups)], axis=1))
    return jnp.concatenate(pieces, axis=0)


def _from_run_major(buf, t):
    tm, width = t.shape
    lane_groups = width // V7X_LANES
    for g in range(tm // SCAN_GROUP):
        for s in range(SCAN_RUN):
            piece = t[g * SCAN_GROUP + s * V7X_SUBLANES:g * SCAN_GROUP + (s + 1) * V7X_SUBLANES]
            rows = pl.ds(g * SCAN_GROUP + s, V7X_SUBLANES, stride=SCAN_RUN)
            for l in range(lane_groups):
                buf[l, rows, :] = piece[:, l * V7X_LANES:(l + 1) * V7X_LANES]
    return jnp.concatenate([buf[l] for l in range(lane_groups)], axis=1)


def _causal_conv_run_major(xp, w_ref, b_ref, carry_ref, cols=slice(None)):
    taps = w_ref.shape[0]
    tm, width = xp.shape
    row = lax.broadcasted_iota(jnp.int32, (V7X_SUBLANES, width), 0)
    wrapped = range(SCAN_RUN - (taps - 1), SCAN_RUN)
    above = {s: carry_ref[s:s + 1, cols] for s in wrapped}
    bias = b_ref[:, cols]
    w = [w_ref[k:k + 1, cols] for k in range(taps)]
    out = []
    for g in range(tm // SCAN_GROUP):
        def piece(s):
            return xp[g * SCAN_GROUP + s * V7X_SUBLANES:g * SCAN_GROUP + (s + 1) * V7X_SUBLANES]

        prev_run = {}
        for s in wrapped:
            prev_run[s] = jnp.where(row >= 1, pltpu.roll(piece(s), 1, 0), above[s])
            above[s] = piece(s)[V7X_SUBLANES - 1:V7X_SUBLANES]
        for s in range(SCAN_RUN):
            acc = bias + w[taps - 1] * piece(s)
            for d in range(1, taps):
                src = piece(s - d) if s >= d else prev_run[s - d + SCAN_RUN]
                acc = acc + w[taps - 1 - d] * src
            out.append(acc)
    for s in wrapped:
        carry_ref[s:s + 1, cols] = above[s]
    return jnp.concatenate(out, axis=0)


def _linear_scan_run_major(a, u, h_in):
    tm = a.shape[0]
    row = lax.broadcasted_iota(jnp.int32, (V7X_SUBLANES, a.shape[1]), 0)
    out = []
    for g in range(tm // SCAN_GROUP):
        h_loc, a_run = [], []
        for s in range(SCAN_RUN):
            rows = slice(g * SCAN_GROUP + s * V7X_SUBLANES, g * SCAN_GROUP + (s + 1) * V7X_SUBLANES)
            h_loc.append(u[rows] if s == 0 else a[rows] * h_loc[-1] + u[rows])
            a_run.append(a[rows] if s == 0 else a[rows] * a_run[-1])
        h_end, a_end = h_loc[-1], a_run[-1]
        d = 1
        while d < V7X_SUBLANES:
            h_end = a_end * _shift_rows(h_end, d, 0.0) + h_end
            a_end = a_end * _shift_rows(a_end, d, 1.0)
            d *= 2
        after = h_end + a_end * h_in
        before = jnp.where(row >= 1, pltpu.roll(after, 1, 0), h_in)
        h_in = after[V7X_SUBLANES - 1:V7X_SUBLANES]
        out += [h_loc[s] + a_run[s] * before for s in range(SCAN_RUN)]
    return jnp.concatenate(out, axis=0), h_in


def _gelu_tanh(t):
    return 0.5 * t * (1.0 + jnp.tanh(math.sqrt(2.0 / math.pi) * (t + 0.044715 * (t * t * t))))


def _mixer_in_kernel(x_ref, mod_ref, g1_ref, w_in_ref, gq_ref, gk_ref, gmat_ref, cw_ref, cb_ref,
                     wg_ref, bg_ref, lam_ref, go_ref,
                     q_ref, k_ref, v_ref, bias_ref, lru_ref,
                     conv_carry, h_carry, km_all, scan_buf):
    @pl.when(pl.program_id(1) == 0)
    def _():
        conv_carry[...] = jnp.zeros_like(conv_carry)
        h_carry[...] = jnp.zeros_like(h_carry)
        km_all[...] = jnp.zeros_like(km_all)

    blocks_per_step = x_ref.shape[1] // MOBA_BLOCK
    mod = mod_ref[0]
    gain = g1_ref[...] * (1.0 + mod[1:2, :])
    shift = mod[0:1, :]
    normed = []
    for sub in range(blocks_per_step):
        x = x_ref[0, sub * MOBA_BLOCK:(sub + 1) * MOBA_BLOCK, :]
        normed.append(((x * _rms_scale(x)) * gain + shift).astype(BF16))
    for sub in range(blocks_per_step):
        _attention_side(sub, pl.program_id(1) * blocks_per_step + sub, normed[sub], w_in_ref, gq_ref, gk_ref,
                        gmat_ref, q_ref, k_ref, v_ref, bias_ref, km_all)
    for sub in range(blocks_per_step):
        _lru_side(sub, normed[sub], w_in_ref, cw_ref, cb_ref, wg_ref, bg_ref, lam_ref, go_ref, lru_ref,
                  conv_carry, h_carry, scan_buf)


def _in_proj(hb, w_in_ref, j):
    return _dot(hb, w_in_ref[:, j * ATTN_WIDTH:(j + 1) * ATTN_WIDTH])


def _attention_side(sub, blk_id, hb, w_in_ref, gq_ref, gk_ref, gmat_ref, q_ref, k_ref, v_ref, bias_ref, km_all):
    tm = MOBA_BLOCK
    rows = slice(sub * tm, (sub + 1) * tm)
    proj = functools.partial(_in_proj, hb, w_in_ref)
    gmat = gmat_ref[...]

    kn = _head_rmsnorm(proj(1), gmat, gk_ref[...])
    k_ref[0, rows, :] = kn.astype(BF16)
    km_all[pl.ds(blk_id, 1), :] = jnp.mean(kn, axis=0, keepdims=True)

    qn =_head_rmsnorm(proj(0), gmat, gq_ref[...] * (HEAD_DIM ** -0.5))
    q_t = qn.T.astype(BF16)
    q_ref[0, :, rows] = q_t

    nb = km_all.shape[0]
    km = km_all[...].astype(BF16)
    feat = lax.broadcasted_iota(jnp.int32, (V7X_LANES, 1), 0)
    for head in range(ATTN_HEADS):
        p, h = divmod(head, HEADS_PER_PAIR)
        q_pair = q_t[p * V7X_LANES:(p + 1) * V7X_LANES]
        qh = jnp.where((feat >= h * HEAD_DIM) & (feat < (h + 1) * HEAD_DIM), q_pair, jnp.zeros_like(q_pair))
        gate = _dot(km[:, p * V7X_LANES:(p + 1) * V7X_LANES], qh)
        bias_ref[0, head * nb:(head + 1) * nb, rows] = _select_bias(gate, blk_id).astype(BF16)

    v_ref[0, :, rows] = proj(2).T.astype(BF16)


def _lru_side(sub, hb, w_in_ref, cw_ref, cb_ref, wg_ref, bg_ref, lam_ref, go_ref, lru_ref, conv_carry, h_carry,
              scan_buf):
    tm = MOBA_BLOCK
    rows = slice(sub * tm, (sub + 1) * tm)
    proj = functools.partial(_in_proj, hb, w_in_ref)

    xc = _causal_conv_run_major(_to_run_major(scan_buf.at[2 * sub], proj(3)), cw_ref, cb_ref, conv_carry)
    xcb = xc.astype(BF16)
    pre = [_dot(xcb[:, j * V7X_MXU_DIM:(j + 1) * V7X_MXU_DIM], wg_ref[j]) for j in range(LRU_WIDTH // V7X_MXU_DIM)]
    bg = bg_ref[...]
    r = jax.nn.sigmoid(jnp.concatenate([g[:, :V7X_MXU_DIM] for g in pre], axis=1) + bg[:, :LRU_WIDTH])
    i = jax.nn.sigmoid(jnp.concatenate([g[:, V7X_MXU_DIM:] for g in pre], axis=1) + bg[:, LRU_WIDTH:])
    neg_lam = -lam_ref[...]
    softplus = jnp.maximum(neg_lam, 0.0) + jnp.log1p(jnp.exp(-jnp.abs(neg_lam)))
    log_a = r * ((-LRU_C) * softplus)
    a = jnp.exp(log_a)
    th = jnp.tanh(log_a)
    u = jnp.sqrt(-2.0 * th / (1.0 - th)) * (i * xc)
    hs, h_last = _linear_scan_run_major(a, u, h_carry[0:1, :])
    h_carry[0:1, :] = h_last
    hs = _from_run_major(scan_buf.at[2 * sub + 1], hs)

    y = hs * _gelu_tanh(proj(4))
    lru_ref[0, rows, :] = (y * _rms_scale(y) * go_ref[...]).astype(BF16)


def _mixer_in(x, mod, g1, w_in, gq, gk, gmat, cw, cb, wg, bg, lam, go):
    B, S, _ = x.shape
    tm = TM_IN
    assert tm % MOBA_BLOCK == 0, "a grid step covers whole MoBA blocks: the block choice reads the means of earlier blocks"
    nb = S // MOBA_BLOCK
    const = lambda shape: pl.BlockSpec(shape, lambda b, s: (0,) * len(shape))
    tok = lambda w: pl.BlockSpec((1, tm, w), lambda b, s: (b, s, 0))
    feat_major = pl.BlockSpec((1, ATTN_WIDTH, tm), lambda b, s: (b, 0, s))
    return pl.pallas_call(
        _mixer_in_kernel,
        grid=(B, S // tm),
        in_specs=[
            tok(D_MODEL),
            pl.BlockSpec((1, N_MOD, D_MODEL), lambda b, s: (b, 0, 0)),
            const((1, D_MODEL)),
            const((D_MODEL, N_IN)),
            const((1, ATTN_WIDTH)),
            const((1, ATTN_WIDTH)),
            const((V7X_MXU_DIM, V7X_MXU_DIM)),
            const((LRU_CONV, LRU_WIDTH)),
            const((1, LRU_WIDTH)),
            const((LRU_WIDTH // V7X_MXU_DIM, V7X_MXU_DIM, 2 * V7X_MXU_DIM)),
            const((1, 2 * LRU_WIDTH)),
            const((1, LRU_WIDTH)),
            const((1, LRU_WIDTH)),
        ],
        out_specs=[
            feat_major,
            tok(ATTN_WIDTH),
            feat_major,
            pl.BlockSpec((1, ATTN_HEADS * nb, tm), lambda b, s: (b, 0, s)),
            tok(LRU_WIDTH),
        ],
        out_shape=[
            jax.ShapeDtypeStruct((B, ATTN_WIDTH, S), BF16),
            jax.ShapeDtypeStruct((B, S, ATTN_WIDTH), BF16),
            jax.ShapeDtypeStruct((B, ATTN_WIDTH, S), BF16),
            jax.ShapeDtypeStruct((B, ATTN_HEADS * nb, S), BF16),
            jax.ShapeDtypeStruct((B, S, LRU_WIDTH), BF16),
        ],
        scratch_shapes=[
            pltpu.VMEM((V7X_SUBLANES, LRU_WIDTH), F32),
            pltpu.VMEM((V7X_SUBLANES, LRU_WIDTH), F32),
            pltpu.VMEM((nb, ATTN_WIDTH), F32),
            pltpu.VMEM((2 * (tm // MOBA_BLOCK), LRU_WIDTH // V7X_LANES, MOBA_BLOCK, V7X_LANES), F32),
        ],
        compiler_params=pltpu.CompilerParams(
            dimension_semantics=("arbitrary", "arbitrary"), vmem_limit_bytes=V7X_VMEM_LIMIT_BYTES),
        name="mixer_in",
    )(x, mod, g1, w_in, gq, gk, gmat, cw, cb, wg, bg, lam, go)


def _select_bias(gate, n_past):
    nb = gate.shape[0]
    blk = lax.broadcasted_iota(jnp.int32, gate.shape, 0)
    neg_inf = jnp.float32(-jnp.inf)
    g = jnp.where(blk < n_past, gate, neg_inf)
    bias = jnp.full(gate.shape, MASK_NEG, F32)
    for _ in range(MOBA_TOPK):
        mx = jnp.max(g, axis=0, keepdims=True)
        cand = jnp.where((g == mx) & (mx > neg_inf), blk, nb)
        pick = blk == jnp.min(cand, axis=0, keepdims=True)
        bias = jnp.where(pick, 0.0, bias)
        g = jnp.where(pick, neg_inf, g)
    return bias


def _moba_kernel(q_ref, k_ref, v_ref, bias_ref, blk_ref, o_ref, s_even, s_odd, mx_even, mx_odd):
    def tile(i, _):
        _moba_tile(i, q_ref, k_ref, v_ref, bias_ref, blk_ref, o_ref, s_even, s_odd, mx_even, mx_odd)
        return 0

    lax.fori_loop(0, q_ref.shape[2] // MOBA_BLOCK, tile, 0)


def _moba_tile(i, q_ref, k_ref, v_ref, bias_ref, blk_ref, o_ref, s_even, s_odd, mx_even, mx_odd):
    tq = MOBA_BLOCK
    nb = bias_ref.shape[1] // HEADS_PER_PAIR
    feat = lax.broadcasted_iota(jnp.int32, (V7X_LANES, 1), 0)
    start = pl.multiple_of(i * tq, tq)
    q_t = q_ref[0, :, pl.ds(start, tq)]

    def v_ext(h, start, n):
        vh = v_ref[0, h * HEAD_DIM:(h + 1) * HEAD_DIM, pl.ds(start, n)]
        return jnp.concatenate([vh, jnp.ones((V7X_BF16_ROWS_PER_VREG, n), BF16)], axis=0)

    kd = k_ref[0, pl.ds(start, MOBA_BLOCK), :]
    key_pos = lax.broadcasted_iota(jnp.int32, (MOBA_BLOCK, tq), 0)
    qry_pos = lax.broadcasted_iota(jnp.int32, (MOBA_BLOCK, tq), 1)
    feat_pad = jnp.zeros((V7X_LANES - nb, tq), BF16)

    q_heads = [jnp.where((feat >= h * HEAD_DIM) & (feat < (h + 1) * HEAD_DIM), q_t, jnp.zeros_like(q_t))
               for h in range(HEADS_PER_PAIR)]
    qx = jnp.concatenate(
        [jnp.concatenate([q_heads[h], bias_ref[0, h * nb:(h + 1) * nb, pl.ds(start, tq)], feat_pad], axis=0)
         for h in range(HEADS_PER_PAIR)], axis=1)

    ck = KV_CHUNK_BLOCKS * MOBA_BLOCK
    n_chunks = (i + KV_CHUNK_BLOCKS - 1) // KV_CHUNK_BLOCKS

    def block_scores(c, t):
        ks = pl.multiple_of(c * ck + t * MOBA_BLOCK, MOBA_BLOCK)
        kx = jnp.concatenate([k_ref[0, pl.ds(ks, MOBA_BLOCK), :], blk_ref[pl.ds(ks, MOBA_BLOCK), :]], axis=1)
        return _dot(kx, qx)

    def run_step(c, cur, nxt, carry, n_cur=KV_CHUNK_BLOCKS, n_nxt=KV_CHUNK_BLOCKS):
        s_ref, mx_ref = cur
        m = [carry[2 * h] for h in range(HEADS_PER_PAIR)]
        acc = [carry[2 * h + 1] for h in range(HEADS_PER_PAIR)]
        if nxt is None:
            n_nxt = 0
        for t in range(max(n_cur, n_nxt)):
            rows = slice(t * MOBA_BLOCK, (t + 1) * MOBA_BLOCK)
            if t < n_nxt:
                s_nxt = block_scores(c + 1, t)
                nxt[0][rows, :] = s_nxt
                nxt[1][t:t + 1, :] = jnp.max(s_nxt, axis=0, keepdims=True)
            if t >= n_cur:
                continue
            ks = pl.multiple_of(c * ck + t * MOBA_BLOCK, MOBA_BLOCK)
            for h in range(HEADS_PER_PAIR):
                cols = slice(h * tq, (h + 1) * tq)
                m_new = jnp.maximum(m[h], mx_ref[t:t + 1, cols])
                p = jnp.exp(s_ref[rows, cols] - m_new).astype(BF16)
                acc[h] = jnp.exp(m[h] - m_new) * acc[h] + _dot(v_ext(h, ks, MOBA_BLOCK), p)
                m[h] = m_new
        return (m[0], acc[0], m[1], acc[1])

    even, odd = (s_even, mx_even), (s_odd, mx_odd)

    s_own = [jnp.where(key_pos <= qry_pos, _dot(kd, q_heads[h]), MASK_NEG) for h in range(HEADS_PER_PAIR)]
    for t in range(KV_CHUNK_BLOCKS):
        s_first = block_scores(0, t)
        s_even[t * MOBA_BLOCK:(t + 1) * MOBA_BLOCK, :] = s_first
        mx_even[t:t + 1, :] = jnp.max(s_first, axis=0, keepdims=True)
    carry = []
    for h in range(HEADS_PER_PAIR):
        m = jnp.max(s_own[h], axis=0, keepdims=True)
        p = jnp.exp(s_own[h] - m).astype(BF16)
        carry += [m, _dot(v_ext(h, start, MOBA_BLOCK), p)]

    def steps(first, count, carry):
        for j in range(count):
            carry = run_step(first + j, *((even, odd) if j % 2 == 0 else (odd, even)), carry)
        return carry

    prefetching = jnp.maximum(n_chunks - 1, 0)
    quads = prefetching // 4
    carry = lax.fori_loop(0, quads, lambda k, carry: steps(4 * k, 4, carry), tuple(carry))
    carry = lax.fori_loop(0, (prefetching - 4 * quads) // 2, lambda k, carry: steps(4 * quads, 2, carry), carry)

    last_blocks = i - (n_chunks - 1) * KV_CHUNK_BLOCKS

    def last_one(r):
        return lambda: finish(run_step(n_chunks - 1, even, None, carry, n_cur=r))

    def last_two(r):
        def run():
            mid = run_step(n_chunks - 2, even, odd, carry, n_nxt=r)
            return finish(run_step(n_chunks - 1, odd, None, mid, n_cur=r))
        return run

    counts = range(1, KV_CHUNK_BLOCKS + 1)

    def finish(carry):
        heads = []
        for h in range(HEADS_PER_PAIR):
            acc = carry[2 * h + 1]
            heads.append(acc[0:HEAD_DIM] / acc[HEAD_DIM:HEAD_DIM + 1])
        o_ref[0, pl.ds(start, tq), :] = jnp.concatenate(heads, axis=0).T
        return 0

    lax.cond(
        n_chunks % 2 == 1,
        lambda: lax.switch(last_blocks - 1, [last_one(r) for r in counts]),
        lambda: lax.cond(n_chunks > 0,
                         lambda: lax.switch(last_blocks - 1, [last_two(r) for r in counts]),
                         lambda: finish(carry)))


def _moba(q, k, v, bias):
    B, S, _ = k.shape
    nb = S // MOBA_BLOCK
    tq = MOBA_BLOCK
    ck = KV_CHUNK_BLOCKS * MOBA_BLOCK
    block_onehot = (jnp.arange(S)[:, None] // MOBA_BLOCK == jnp.arange(V7X_LANES)[None, :]).astype(BF16)
    feat_major = pl.BlockSpec((1, V7X_LANES, S), lambda b, p: (b, p, 0))
    return pl.pallas_call(
        _moba_kernel,
        grid=(B, N_PAIRS),
        in_specs=[
            feat_major,
            pl.BlockSpec((1, S, V7X_LANES), lambda b, p: (b, 0, p)),
            feat_major,
            pl.BlockSpec((1, HEADS_PER_PAIR * nb, S), lambda b, p: (b, p, 0)),
            pl.BlockSpec((S, V7X_LANES), lambda b, p: (0, 0), pipeline_mode=pl.Buffered(1)),
        ],
        out_specs=pl.BlockSpec((1, S, V7X_LANES), lambda b, p: (b, 0, p)),
        out_shape=jax.ShapeDtypeStruct((B, S, ATTN_WIDTH), F32),
        scratch_shapes=[
            pltpu.VMEM((ck, HEADS_PER_PAIR * tq), F32),
            pltpu.VMEM((ck, HEADS_PER_PAIR * tq), F32),
            pltpu.VMEM((V7X_SUBLANES, HEADS_PER_PAIR * tq), F32),
            pltpu.VMEM((V7X_SUBLANES, HEADS_PER_PAIR * tq), F32),
        ],
        compiler_params=pltpu.CompilerParams(
            dimension_semantics=("arbitrary", "arbitrary"),
            vmem_limit_bytes=V7X_VMEM_LIMIT_BYTES),
        name="moba",
    )(q, k, v, bias, block_onehot)


def _mixer_out_kernel(x_ref, lru_ref, attn_ref, mod_ref, ga_ref, w_out_ref, g2_ref, w_up_ref, cw_ref, cb_ref,
                      w_down_ref, o_ref, conv_carry, perm_in, perm_out):
    tm = x_ref.shape[1]

    @pl.when(pl.program_id(1) == 0)
    def _():
        conv_carry[...] = jnp.zeros_like(conv_carry)

    mod = mod_ref[0]
    gate1, sh2, sc2, gate2 = mod[2:3, :], mod[3:4, :], mod[4:5, :], mod[5:6, :]

    gain2 = g2_ref[...] * (1.0 + sc2)
    x1_parts, h2_parts = [], []
    for r0 in range(0, tm, tm // 2):
        rows = slice(r0, r0 + tm // 2)
        attn = attn_ref[0, rows, :]
        attn_n = (attn * _rms_scale(attn) * ga_ref[...]).astype(BF16)
        mixed = _dot(lru_ref[0, rows, :], w_out_ref[0:LRU_WIDTH, :]) + _dot(attn_n, w_out_ref[LRU_WIDTH:, :])
        x1_parts.append(x_ref[0, rows, :] + gate1 * mixed)
        h2_parts.append((x1_parts[-1] * _rms_scale(x1_parts[-1])) * gain2 + sh2)
    x1 = jnp.concatenate(x1_parts, axis=0)

    h2 = _to_run_major(perm_in, jnp.concatenate(h2_parts, axis=0)).astype(BF16)

    up_g = _dot(h2, w_up_ref[:, 0:D_FF])
    up_v = _dot(h2, w_up_ref[:, D_FF:2 * D_FF])
    acts = []
    for c0 in range(0, D_FF, D_FF // 2):
        g = _causal_conv_run_major(up_g[:, c0:c0 + D_FF // 2], cw_ref, cb_ref, conv_carry,
                                   slice(c0, c0 + D_FF // 2))
        val = _causal_conv_run_major(up_v[:, c0:c0 + D_FF // 2], cw_ref, cb_ref, conv_carry,
                                     slice(D_FF + c0, D_FF + c0 + D_FF // 2))
        acts.append((g * jax.nn.sigmoid(g) * val).astype(BF16))
    act = jnp.concatenate(acts, axis=1)
    o_ref[0] = x1 + gate2 * _from_run_major(perm_out, _dot(act, w_down_ref[...]))


def _mixer_out(x, lru, attn, mod, ga, w_out, g2, w_up, cw, cb, w_down):
    B, S, _ = x.shape
    tm = TM_OUT
    const = lambda shape: pl.BlockSpec(shape, lambda b, s: (0,) * len(shape), pipeline_mode=pl.Buffered(1))
    tok = lambda w: pl.BlockSpec((1, tm, w), lambda b, s: (b, s, 0))
    return pl.pallas_call(
        _mixer_out_kernel,
        grid=(B, S // tm),
        in_specs=[
            tok(D_MODEL),
            tok(LRU_WIDTH),
            tok(ATTN_WIDTH),
            pl.BlockSpec((1, N_MOD, D_MODEL), lambda b, s: (b, 0, 0)),
            const((1, ATTN_WIDTH)),
            const((D_MODEL, D_MODEL)),
            const((1, D_MODEL)),
            const((D_MODEL, 2 * D_FF)),
            const((FFN_CONV, 2 * D_FF)),
            const((1, 2 * D_FF)),
            const((D_FF, D_MODEL)),
        ],
        out_specs=tok(D_MODEL),
        out_shape=jax.ShapeDtypeStruct((B, S, D_MODEL), F32),
        scratch_shapes=[
            pltpu.VMEM((V7X_SUBLANES, 2 * D_FF), F32),
            pltpu.VMEM((D_MODEL // V7X_LANES, tm, V7X_LANES), F32),
            pltpu.VMEM((D_MODEL // V7X_LANES, tm, V7X_LANES), F32),
        ],
        compiler_params=pltpu.CompilerParams(
            dimension_semantics=("arbitrary", "arbitrary"), vmem_limit_bytes=V7X_VMEM_LIMIT_BYTES),
        name="mixer_out",
    )(x, lru, attn, mod, ga, w_out, g2, w_up, cw, cb, w_down)


def _block_diag(w):
    n, d, _ = w.shape
    eye = jnp.eye(n, dtype=w.dtype)
    return jnp.einsum('hij,hg->higj', w, eye).reshape(n * d, n * d)


def _layer(x, c, w_ada, b_ada, norm1_g, w_in, q_norm_g, k_norm_g, lru_conv_w, lru_conv_b, lru_wa, lru_ba,
           lru_wx, lru_bx, lru_lambda, lru_out_g, attn_out_g, w_out, norm2_g, w_up, ffn_conv_w, ffn_conv_b,
           w_down):
    B, S, _ = x.shape
    row = lambda t: t.reshape(1, -1)
    mod = _modulation(c, w_ada, row(b_ada)).reshape(B, N_MOD, D_MODEL)

    head_id = jnp.arange(V7X_MXU_DIM) // HEAD_DIM
    gmat = jnp.where(head_id[:, None] == head_id[None, :], 1.0 / HEAD_DIM, 0.0).astype(BF16)
    heads_per_tile = V7X_MXU_DIM // (LRU_WIDTH // LRU_HEADS)
    w_gates = jnp.stack([
        jnp.concatenate([_block_diag(w[j * heads_per_tile:(j + 1) * heads_per_tile]) for w in (lru_wa, lru_wx)], axis=1)
        for j in range(LRU_HEADS // heads_per_tile)]).astype(BF16)
    b_gates = jnp.concatenate([lru_ba, lru_bx]).reshape(1, -1)

    q, k, v, bias, lru = _mixer_in(
        x, mod, row(norm1_g), w_in.astype(BF16), row(jnp.tile(q_norm_g, ATTN_HEADS)),
        row(jnp.tile(k_norm_g, ATTN_HEADS)), gmat, lru_conv_w, row(lru_conv_b), w_gates, b_gates,
        row(lru_lambda), row(lru_out_g))
    attn = _moba(q, k, v, bias)
    return _mixer_out(x, lru, attn, mod, row(attn_out_g), w_out.astype(BF16), row(norm2_g), w_up.astype(BF16),
                      ffn_conv_w, row(ffn_conv_b), w_down.astype(BF16))


def kernel(x, c, w_ada, b_ada, norm1_g, w_in, q_norm_g, k_norm_g, lru_conv_w, lru_conv_b, lru_wa, lru_ba, lru_wx, lru_bx, lru_lambda, lru_out_g, attn_out_g, w_out, norm2_g, w_up, ffn_conv_w, ffn_conv_b, w_down):
    depth = w_ada.shape[0]
    for l in range(depth):
        x = _layer(x, c, w_ada[l], b_ada[l], norm1_g[l], w_in[l], q_norm_g[l], k_norm_g[l], lru_conv_w[l],
                   lru_conv_b[l], lru_wa[l], lru_ba[l], lru_wx[l], lru_bx[l], lru_lambda[l], lru_out_g[l],
                   attn_out_g[l], w_out[l], norm2_g[l], w_up[l], ffn_conv_w[l], ffn_conv_b[l], w_down[l])
    return x
```
